```python
import jax, jax.numpy as jnp
from jax import lax
import numpy as np

D_MODEL = 1024
BATCH = 16
SEQ = 2048
DEPTH = 4

SSM_EXPAND = 2
SSM_WIDTH = SSM_EXPAND * D_MODEL
SSM_HEAD_DIM = 64
SSM_HEADS = SSM_WIDTH // SSM_HEAD_DIM
SSM_GROUPS = 2
SSM_STATE = 128
CONV_WIDTH = 4
CHUNK = 128
CONV_CH = SSM_WIDTH + 2 * SSM_GROUPS * SSM_STATE
POOL_WIDTH = D_MODEL
POOL_WINDOWS = (2, 4, 8, 16)
POOL_GROUPS = 4
POOL_GROUP_DIM = POOL_WIDTH // POOL_GROUPS
SB_WIDTH = D_MODEL
SB_HEAD_DIM = 64
SB_HEADS = SB_WIDTH // SB_HEAD_DIM
SB_BLOCK = 128
N_BRANCHES = 3
EPS = 1e-6
IN_SIZES = (SSM_WIDTH, CONV_CH, SSM_HEADS, POOL_WIDTH, POOL_WIDTH, 3 * SB_WIDTH, SB_WIDTH, N_BRANCHES * D_MODEL)
IN_COLS = SSM_WIDTH + CONV_CH + SSM_HEADS + 2 * POOL_WIDTH + 4 * SB_WIDTH + N_BRANCHES * D_MODEL

kernel_name = "hybrid_ssd_pool_stickbreak_gated_block"


def _split_points():
    pts, run = [], 0
    for s in IN_SIZES[:-1]:
        run += s
        pts.append(run)
    return pts


def rms_norm(x, w):
    xf = x.astype(jnp.float32)
    var = jnp.mean(xf * xf, axis=-1, keepdims=True)
    return (xf * lax.rsqrt(var + EPS)).astype(x.dtype) * w


def causal_dwconv(u, w, b):
    S = u.shape[1]
    up = jnp.pad(u, ((0, 0), (CONV_WIDTH - 1, 0), (0, 0)))
    out = b
    for k in range(CONV_WIDTH):
        out = out + up[:, k:k + S] * w[k]
    return out


def segsum(a):
    T = a.shape[-1]
    cs = jnp.cumsum(a, axis=-1)
    seg = cs[..., :, None] - cs[..., None, :]
    mask = jnp.tril(jnp.ones((T, T), dtype=bool))
    return jnp.where(mask, seg, -jnp.inf)


def ssd_chunked(xh, dt, a, Bg, Cg):
    Bsz, S, H, P = xh.shape
    G, N = Bg.shape[2], Bg.shape[3]
    hpg = H // G
    nc = S // CHUNK
    xdt = (xh * dt[..., None]).reshape(Bsz, nc, CHUNK, G, hpg, P)
    adt = (dt.astype(jnp.float32) * a.astype(jnp.float32)).reshape(Bsz, nc, CHUNK, G, hpg)
    adt = jnp.moveaxis(adt, 2, -1)
    Bc = Bg.reshape(Bsz, nc, CHUNK, G, N)
    Cc = Cg.reshape(Bsz, nc, CHUNK, G, N)
    a_cum = jnp.cumsum(adt, axis=-1)
    decay_in = jnp.exp(segsum(adt))
    cb = jnp.einsum('bclgn,bcsgn->bcgls', Cc, Bc)
    y_diag = jnp.einsum('bcghls,bcsghp->bclghp', cb[:, :, :, None] * decay_in, xdt)
    decay_states = jnp.exp(a_cum[..., -1:] - a_cum)
    states = jnp.einsum('bclgn,bcghl,bclghp->bcghpn', Bc, decay_states, xdt)
    chunk_decay = jnp.exp(a_cum[..., -1])

    def step(carry, inp):
        st, dec = inp
        return carry * dec[..., None, None] + st, carry

    init = jnp.zeros_like(states[:, 0])
    _, prev = lax.scan(step, init, (jnp.moveaxis(states, 1, 0), jnp.moveaxis(chunk_decay, 1, 0)))
    prev = jnp.moveaxis(prev, 0, 1)
    y_off = jnp.einsum('bclgn,bcghpn,bcghl->bclghp', Cc, prev, jnp.exp(a_cum))
    return (y_diag + y_off).reshape(Bsz, S, H, P)


def mamba2_branch(z, xbc, dt_raw, conv_w, conv_b, dt_bias, a_log, d_skip, ssm_norm_w):
    Bsz, S, _ = z.shape
    xbc = jax.nn.silu(causal_dwconv(xbc, conv_w, conv_b))
    xs, Bg, Cg = jnp.split(xbc, [SSM_WIDTH, SSM_WIDTH + SSM_GROUPS * SSM_STATE], axis=-1)
    xh = xs.reshape(Bsz, S, SSM_HEADS, SSM_HEAD_DIM)
    Bg = Bg.reshape(Bsz, S, SSM_GROUPS, SSM_STATE)
    Cg = Cg.reshape(Bsz, S, SSM_GROUPS, SSM_STATE)
    dt = jax.nn.softplus((dt_raw + dt_bias).astype(jnp.float32))
    a = -jnp.exp(a_log.astype(jnp.float32))
    y = ssd_chunked(xh, dt, a, Bg, Cg) + xh * d_skip[:, None]
    y = y.reshape(Bsz, S, SSM_WIDTH).astype(z.dtype)
    return rms_norm(y * jax.nn.silu(z), ssm_norm_w)


def pool_branch(u, gate, pool_w, pool_scale):
    Bsz, S, _ = u.shape
    uf = u.astype(jnp.float32).reshape(Bsz, S, POOL_GROUPS, POOL_GROUP_DIM)
    cs = jnp.cumsum(uf, axis=1)
    pos = jnp.arange(S)
    pooled = []
    for g, w in enumerate(POOL_WINDOWS):
        csw = jnp.pad(cs[:, :, g], ((0, 0), (w, 0), (0, 0)))
        win_sum = csw[:, w:] - csw[:, :S]
        cnt = jnp.minimum(pos + 1, w).astype(jnp.float32)
        pooled.append(win_sum / cnt[None, :, None])
    mixed = jnp.stack(pooled, axis=2) - uf
    mixed = jnp.einsum('bsgi,gio->bsgo', mixed.astype(u.dtype), pool_w).reshape(Bsz, S, POOL_WIDTH)
    return (mixed * pool_scale * jax.nn.silu(gate)).astype(u.dtype)


def stick_breaking_branch(qkv, gate):
    Bsz, S, _ = qkv.shape
    q, k, v = jnp.split(qkv, 3, axis=-1)

    def heads(t):
        return t.reshape(Bsz, S, SB_HEADS, SB_HEAD_DIM).transpose(0, 2, 1, 3)

    q, k, v = heads(q), heads(k), heads(v)
    scale = SB_HEAD_DIM ** -0.5
    outs = []
    for i in range(S // SB_BLOCK):
        q0 = i * SB_BLOCK
        kend = q0 + SB_BLOCK
        qb = q[:, :, q0:kend]
        kb = k[:, :, :kend]
        vb = v[:, :, :kend]
        z = jnp.einsum('bhtd,bhsd->bhts', qb, kb).astype(jnp.float32) * scale
        causal = (q0 + jnp.arange(SB_BLOCK))[:, None] > jnp.arange(kend)[None, :]
        log_beta = jax.nn.log_sigmoid(z)
        log_one_minus = jnp.where(causal, jax.nn.log_sigmoid(-z), 0.0)
        later = lax.cumsum(log_one_minus, axis=3, reverse=True) - log_one_minus
        att = jnp.where(causal, jnp.exp(log_beta + later), 0.0)
        outs.append(jnp.einsum('bhts,bhsd->bhtd', att.astype(vb.dtype), vb))
    o = jnp.concatenate(outs, axis=2).transpose(0, 2, 1, 3).reshape(Bsz, S, SB_WIDTH)
    return o * jax.nn.silu(gate)


def hybrid_layer(x, norm_w, w_in, conv_w, conv_b, dt_bias, a_log, d_skip, ssm_norm_w,
                 pool_w, pool_scale, w_proj_ssm, w_proj_pool, w_proj_sb, w_out):
    Bsz, S, D = x.shape
    h = rms_norm(x, norm_w)
    proj = h @ w_in
    z, xbc, dt_raw, pool_u, pool_gate, qkv, sb_gate, merge = jnp.split(proj, _split_points(), axis=-1)
    y_ssm = mamba2_branch(z, xbc, dt_raw, conv_w, conv_b, dt_bias, a_log, d_skip, ssm_norm_w) @ w_proj_ssm
    y_pool = pool_branch(pool_u, pool_gate, pool_w, pool_scale) @ w_proj_pool
    y_sb = stick_breaking_branch(qkv, sb_gate) @ w_proj_sb
    g = jax.nn.sigmoid(merge.astype(jnp.float32)).reshape(Bsz, S, N_BRANCHES, D).astype(x.dtype)
    merged = g[:, :, 0] * y_ssm + g[:, :, 1] * y_pool + g[:, :, 2] * y_sb
    return x + (merged @ w_out).astype(x.dtype)


def setup_inputs(seed: int = 0) -> dict:
    key = jax.random.key(seed)
    ks = jax.random.split(key, 17)
    f32 = jnp.float32
    nrm = lambda k, shape, s: jax.random.normal(k, shape, f32) * s
    dt = jnp.exp(jax.random.uniform(ks[5], (DEPTH, SSM_HEADS), f32, float(np.log(1e-3)), float(np.log(1e-1))))
    return {
        "x": nrm(ks[0], (BATCH, SEQ, D_MODEL), 1.0),
        "norm_w": 1.0 + nrm(ks[1], (DEPTH, D_MODEL), 0.02),
        "w_in": nrm(ks[2], (DEPTH, D_MODEL, IN_COLS), D_MODEL ** -0.5),
        "conv_w": nrm(ks[3], (DEPTH, CONV_WIDTH, CONV_CH), CONV_WIDTH ** -0.5),
        "conv_b": nrm(ks[4], (DEPTH, CONV_CH), 0.02),
        "dt_bias": dt + jnp.log(-jnp.expm1(-dt)),
        "a_log": jnp.log(jax.random.uniform(ks[6], (DEPTH, SSM_HEADS), f32, 1.0, 16.0)),
        "d_skip": 1.0 + nrm(ks[7], (DEPTH, SSM_HEADS), 0.02),
        "ssm_norm_w": 1.0 + nrm(ks[8], (DEPTH, SSM_WIDTH), 0.02),
        "pool_w": nrm(ks[9], (DEPTH, POOL_GROUPS, POOL_GROUP_DIM, POOL_GROUP_DIM), POOL_GROUP_DIM ** -0.5),
        "pool_scale": 1.0 + nrm(ks[10], (DEPTH, POOL_WIDTH), 0.02),
        "w_proj_ssm": nrm(ks[11], (DEPTH, SSM_WIDTH, D_MODEL), SSM_WIDTH ** -0.5),
        "w_proj_pool": nrm(ks[12], (DEPTH, POOL_WIDTH, D_MODEL), POOL_WIDTH ** -0.5),
        "w_proj_sb": nrm(ks[13], (DEPTH, SB_WIDTH, D_MODEL), SB_WIDTH ** -0.5),
        "w_out": nrm(ks[14], (DEPTH, D_MODEL, D_MODEL), (N_BRANCHES * D_MODEL) ** -0.5),
        "final_norm_w": 1.0 + nrm(ks[15], (D_MODEL,), 0.02),
    }


def reference(x, norm_w, w_in, conv_w, conv_b, dt_bias, a_log, d_skip, ssm_norm_w,
              pool_w, pool_scale, w_proj_ssm, w_proj_pool, w_proj_sb, w_out, final_norm_w):
    for l in range(DEPTH):
        x = hybrid_layer(x, norm_w[l], w_in[l], conv_w[l], conv_b[l], dt_bias[l], a_log[l],
                         d_skip[l], ssm_norm_w[l], pool_w[l], pool_scale[l], w_proj_ssm[l],
                         w_proj_pool[l], w_proj_sb[l], w_out[l])
    return rms_norm(x, final_norm_w)
```

```python
import functools

import jax
import jax.numpy as jnp
from jax import lax
from jax.experimental import pallas as pl
from jax.experimental.pallas import tpu as pltpu

F32 = jnp.float32
BF16 = jnp.bfloat16

D_MODEL = 1024
SSM_WIDTH = 2048
SSM_HEAD_DIM = 64
SSM_HEADS = 32
SSM_GROUPS = 2
SSM_STATE = 128
CONV_WIDTH = 4
CHUNK = 128
BC_WIDTH = 2 * SSM_GROUPS * SSM_STATE
POOL_WIDTH = 1024
POOL_WINDOWS = (2, 4, 8, 16)
POOL_GROUP_DIM = 256
POOL_HALO = 16
SB_WIDTH = 1024
SB_HEAD_DIM = 64
EPS = 1e-6
LANES = 128
CONV_HALO = 8

COL_Z, COL_XS, COL_PU, COL_PG, COL_SG, COL_M0, COL_BC = 0, 2048, 4096, 5120, 6144, 7168, 10240
P_COLS = 10752
QKV_COLS = 3072

VMEM_LIMIT = 56 * 1024 * 1024


def _cparams(sem):
    return pltpu.CompilerParams(dimension_semantics=sem, vmem_limit_bytes=VMEM_LIMIT)


def _sigmoid(x):
    return 1.0 / (1.0 + jnp.exp(-x))


def _silu(x):
    return x * _sigmoid(x)


def _softplus(x):
    return jnp.maximum(x, 0.0) + jnp.log1p(jnp.exp(-jnp.abs(x)))


def _split_bf16(x, n):
    parts, r = [], x
    for _ in range(n):
        p = r.astype(BF16)
        parts.append(p)
        r = r - p.astype(F32)
    return parts


def _inproj_kernel(x_ref, nw_ref, w_ref, *rest, with_dt):
    if with_dt:
        wdt_ref, o_ref, dt_ref, h_ref = rest
    else:
        o_ref, h_ref = rest

    @pl.when(pl.program_id(1) == 0)
    def _():
        x = x_ref[...]
        var = jnp.mean(x * x, axis=-1, keepdims=True)
        h = (x * lax.rsqrt(var + EPS)) * nw_ref[...]
        hb = h.astype(BF16)
        h_ref[...] = hb
        if with_dt:
            dt_ref[...] = jnp.dot(hb, wdt_ref[...], preferred_element_type=F32)

    o_ref[...] = jnp.dot(h_ref[...], w_ref[...], preferred_element_type=F32).astype(o_ref.dtype)


def _inproj(x2d, nw, w, wdt, out_dtype, tm, tn):
    M, D = x2d.shape
    N = w.shape[1]
    with_dt = wdt is not None
    in_specs = [
        pl.BlockSpec((tm, D), lambda i, j: (i, 0)),
        pl.BlockSpec((1, D), lambda i, j: (0, 0)),
        pl.BlockSpec((D, tn), lambda i, j: (0, j)),
    ]
    out_shape = [jax.ShapeDtypeStruct((M, N), out_dtype)]
    out_specs = [pl.BlockSpec((tm, tn), lambda i, j: (i, j))]
    args = [x2d, nw, w]
    if with_dt:
        in_specs.append(pl.BlockSpec((D, LANES), lambda i, j: (0, 0)))
        out_shape.append(jax.ShapeDtypeStruct((M, LANES), F32))
        out_specs.append(pl.BlockSpec((tm, LANES), lambda i, j: (i, 0)))
        args.append(wdt)
    res = pl.pallas_call(
        functools.partial(_inproj_kernel, with_dt=with_dt),
        grid=(M // tm, N // tn),
        in_specs=in_specs,
        out_specs=out_specs,
        out_shape=out_shape,
        scratch_shapes=[pltpu.VMEM((tm, D), BF16)],
        compiler_params=_cparams(("parallel", "arbitrary")),
        name="inproj_dt" if with_dt else "inproj_qkv",
    )(*args)
    return res


def _ssm_kernel(z_ref, xs_ref, bc_ref, dt_ref, cwx_ref, cbx_ref, cwbc_ref, cbbc_ref, dtb_ref,
                alog_ref, dskip_ref, nw_ref, tri_ref, exp_ref, o_ref,
                extx_ref, extbc_ref, state_ref):
    c = pl.program_id(1)
    L = CHUNK
    H0 = CONV_HALO

    @pl.when(c == 0)
    def _():
        extx_ref[0:H0, :] = jnp.zeros((H0, SSM_WIDTH), F32)
        extbc_ref[0:H0, :] = jnp.zeros((H0, BC_WIDTH), F32)
        state_ref[...] = jnp.zeros_like(state_ref)

    def conv(u_ref, ext_ref, w_ref, b_ref):
        ext_ref[H0:H0 + L, :] = u_ref[0]
        acc = b_ref[...] + w_ref[CONV_WIDTH - 1:CONV_WIDTH, :] * ext_ref[H0:H0 + L, :]
        for k in range(CONV_WIDTH - 1):
            s = CONV_WIDTH - 1 - k
            acc = acc + w_ref[k:k + 1, :] * ext_ref[H0 - s:H0 - s + L, :]
        ext_ref[0:H0, :] = ext_ref[L:L + H0, :]
        return _silu(acc)

    xs = conv(xs_ref, extx_ref, cwx_ref, cbx_ref)
    bc = conv(bc_ref, extbc_ref, cwbc_ref, cbbc_ref)

    dt = _softplus(dt_ref[0] + dtb_ref[...])
    a = -jnp.exp(alog_ref[...])
    adt = dt * a
    tri = tri_ref[...]
    a_cum = sum(jnp.dot(tri, p, preferred_element_type=F32) for p in _split_bf16(adt, 3))
    a_cum_t = a_cum.T
    a_last = a_cum[L - 1:L, :]

    stack = jnp.concatenate([dt, jnp.exp(a_last - a_cum), jnp.exp(a_cum)], axis=0)
    hi, lo = _split_bf16(stack, 2)
    expanded = (jnp.dot(hi, exp_ref[...], preferred_element_type=F32)
                + jnp.dot(lo, exp_ref[...], preferred_element_type=F32))
    dt_x = expanded[0:L]
    dec_x = expanded[L:2 * L]
    expa_x = expanded[2 * L:3 * L]

    xdt = xs * dt_x
    dxs = (xdt * dec_x).astype(BF16)

    row = lax.broadcasted_iota(jnp.int32, (L, L), 0)
    col = lax.broadcasted_iota(jnp.int32, (L, L), 1)
    causal = row >= col
    lane = lax.broadcasted_iota(jnp.int32, (L, LANES), 1)
    first_half = lane < SSM_HEAD_DIM

    hpg = SSM_HEADS // SSM_GROUPS
    gw = hpg * SSM_HEAD_DIM
    y_parts = []
    for g in range(SSM_GROUPS):
        b_g = bc[:, g * SSM_STATE:(g + 1) * SSM_STATE]
        c_g = bc[:, (SSM_GROUPS + g) * SSM_STATE:(SSM_GROUPS + g + 1) * SSM_STATE].astype(BF16)
        cb = lax.dot_general(c_g, b_g.astype(BF16), (((1,), (1,)), ((), ())),
                             preferred_element_type=F32)
        for jj in range(hpg // 2):
            pj = g * (hpg // 2) + jj
            ms = []
            for h in (2 * pj, 2 * pj + 1):
                seg = a_cum[:, h:h + 1] - a_cum_t[h:h + 1, :]
                decay = jnp.where(causal, jnp.exp(jnp.where(causal, seg, 0.0)), 0.0)
                ms.append((cb * decay).astype(BF16))
            lhs = jnp.concatenate(ms, axis=1)
            xp = xdt[:, pj * LANES:(pj + 1) * LANES]
            rhs = jnp.concatenate([jnp.where(first_half, xp, 0.0).astype(BF16),
                                   jnp.where(first_half, 0.0, xp).astype(BF16)], axis=0)
            y_parts.append(jnp.dot(lhs, rhs, preferred_element_type=F32))
    y = jnp.concatenate(y_parts, axis=1)

    offs = []
    for g in range(SSM_GROUPS):
        b_g_t = bc[:, g * SSM_STATE:(g + 1) * SSM_STATE].T.astype(BF16)
        c_g = bc[:, (SSM_GROUPS + g) * SSM_STATE:(SSM_GROUPS + g + 1) * SSM_STATE].astype(BF16)
        prev = state_ref[g]
        offs.append(jnp.dot(c_g, prev.astype(BF16), preferred_element_type=F32))
        new = jnp.dot(b_g_t, dxs[:, g * gw:(g + 1) * gw], preferred_element_type=F32)
        state_ref[g] = prev * expa_x[L - 1:L, g * gw:(g + 1) * gw] + new
    y = y + jnp.concatenate(offs, axis=1) * expa_x + xs * dskip_ref[...]

    v = y * _silu(z_ref[0])
    var = jnp.mean(v * v, axis=-1, keepdims=True)
    o_ref[0] = ((v * lax.rsqrt(var + EPS)) * nw_ref[...]).astype(o_ref.dtype)


def _ssm_branch(P, dtraw, cw, cb, dtb, alog, dskip_x, nw, tri, expand):
    B, S, _ = P.shape
    L = CHUNK
    cwx, cwbc = cw[:, :SSM_WIDTH], cw[:, SSM_WIDTH:]
    cbx, cbbc = cb[:, :SSM_WIDTH], cb[:, SSM_WIDTH:]
    const = lambda shape: pl.BlockSpec(shape, lambda b, c: (0,) * len(shape))
    return pl.pallas_call(
        _ssm_kernel,
        grid=(B, S // L),
        in_specs=[
            pl.BlockSpec((1, L, SSM_WIDTH), lambda b, c: (b, c, COL_Z // SSM_WIDTH)),
            pl.BlockSpec((1, L, SSM_WIDTH), lambda b, c: (b, c, COL_XS // SSM_WIDTH)),
            pl.BlockSpec((1, L, BC_WIDTH), lambda b, c: (b, c, COL_BC // BC_WIDTH)),
            pl.BlockSpec((1, L, LANES), lambda b, c: (b, c, 0)),
            const((CONV_WIDTH, SSM_WIDTH)), const((1, SSM_WIDTH)),
            const((CONV_WIDTH, BC_WIDTH)), const((1, BC_WIDTH)),
            const((1, LANES)), const((1, LANES)), const((1, SSM_WIDTH)), const((1, SSM_WIDTH)),
            const((L, L)), const((LANES, SSM_WIDTH)),
        ],
        out_specs=pl.BlockSpec((1, L, SSM_WIDTH), lambda b, c: (b, c, 0)),
        out_shape=jax.ShapeDtypeStruct((B, S, SSM_WIDTH), BF16),
        scratch_shapes=[
            pltpu.VMEM((CONV_HALO + L, SSM_WIDTH), F32),
            pltpu.VMEM((CONV_HALO + L, BC_WIDTH), F32),
            pltpu.VMEM((SSM_GROUPS, SSM_STATE, SSM_WIDTH // SSM_GROUPS), F32),
        ],
        compiler_params=_cparams(("parallel", "arbitrary")),
        name="ssm_branch",
    )(P, P, P, dtraw, cwx, cbx, cwbc, cbbc, dtb, alog, dskip_x, nw, tri, expand)


def _pool_kernel(u_ref, g_ref, w_ref, sc_ref, o_ref, ext_ref, *, T):
    t = pl.program_id(1)
    H0 = POOL_HALO

    @pl.when(t == 0)
    def _():
        ext_ref[0:H0, :] = jnp.zeros((H0, POOL_WIDTH), F32)

    ext_ref[H0:H0 + T, :] = u_ref[0]
    pos = t * T + lax.broadcasted_iota(jnp.int32, (T, 1), 0)
    outs = []
    for g, w in enumerate(POOL_WINDOWS):
        cs = slice(g * POOL_GROUP_DIM, (g + 1) * POOL_GROUP_DIM)
        u = ext_ref[H0:H0 + T, cs]
        win = u
        for k in range(1, w):
            win = win + ext_ref[H0 - k:H0 - k + T, cs]
        cnt = jnp.minimum(pos + 1, w).astype(F32)
        mixed = (win / cnt - u).astype(BF16)
        outs.append(jnp.dot(mixed, w_ref[g], preferred_element_type=F32))
    ext_ref[0:H0, :] = ext_ref[T:T + H0, :]
    mixed = jnp.concatenate(outs, axis=1)
    o_ref[0] = (mixed * sc_ref[...] * _silu(g_ref[0])).astype(o_ref.dtype)


def _pool_branch(P, pool_w, pool_scale, T):
    B, S, _ = P.shape
    return pl.pallas_call(
        functools.partial(_pool_kernel, T=T),
        grid=(B, S // T),
        in_specs=[
            pl.BlockSpec((1, T, POOL_WIDTH), lambda b, t: (b, t, COL_PU // POOL_WIDTH)),
            pl.BlockSpec((1, T, POOL_WIDTH), lambda b, t: (b, t, COL_PG // POOL_WIDTH)),
            pl.BlockSpec((len(POOL_WINDOWS), POOL_GROUP_DIM, POOL_GROUP_DIM), lambda b, t: (0, 0, 0)),
            pl.BlockSpec((1, POOL_WIDTH), lambda b, t: (0, 0)),
        ],
        out_specs=pl.BlockSpec((1, T, POOL_WIDTH), lambda b, t: (b, t, 0)),
        out_shape=jax.ShapeDtypeStruct((B, S, POOL_WIDTH), BF16),
        scratch_shapes=[pltpu.VMEM((POOL_HALO + T, POOL_WIDTH), F32)],
        compiler_params=_cparams(("parallel", "arbitrary")),
        name="pool_branch",
    )(P, P, pool_w, pool_scale)


def _attn_kernel(q_ref, k_ref, v_ref, g_ref, tri_ref, o_ref, *, T):
    i = pl.program_id(2)
    lane = lax.broadcasted_iota(jnp.int32, (T, LANES), 1)
    first = lane < SB_HEAD_DIM
    q = q_ref[0].astype(F32) * (SB_HEAD_DIM ** -0.5)
    q_heads = (jnp.where(first, q, 0.0).astype(BF16), jnp.where(first, 0.0, q).astype(BF16))
    tri2 = tri_ref[...]
    row = lax.broadcasted_iota(jnp.int32, (T, T), 0)
    col = lax.broadcasted_iota(jnp.int32, (T, T), 1)
    strict = row > col

    def tile(j, carries, accs, diag):
        start = pl.multiple_of(j * T, T)
        ks = k_ref[0, pl.ds(start, T), :]
        vs = v_ref[0, pl.ds(start, T), :]
        new_c, new_a = [], []
        for qh, carry, acc in zip(q_heads, carries, accs):
            z = lax.dot_general(qh, ks, (((1,), (1,)), ((), ())), preferred_element_type=F32)
            sp = jnp.log1p(jnp.exp(-jnp.abs(z)))
            lb = jnp.minimum(z, 0.0) - sp
            l1m = lb - z
            if diag:
                l1m = jnp.where(strict, l1m, 0.0)
            hi, lo = _split_bf16(l1m, 2)
            later = jnp.dot(jnp.concatenate([hi, lo], axis=1), tri2, preferred_element_type=F32)
            att = jnp.exp(lb + later + carry)
            if diag:
                att = jnp.where(strict, att, 0.0)
            new_a.append(acc + jnp.dot(att.astype(BF16), vs, preferred_element_type=F32))
            new_c.append(carry + jnp.sum(l1m, axis=1, keepdims=True))
        return tuple(new_c), tuple(new_a)

    zc = jnp.zeros((T, 1), F32)
    za = jnp.zeros((T, LANES), F32)
    carries, accs = tile(i, (zc, zc), (za, za), True)

    def body(t, st):
        return tile(i - 1 - t, st[0], st[1], False)

    carries, accs = lax.fori_loop(0, i, body, (carries, accs))
    o = jnp.where(first, accs[0], accs[1])
    o_ref[0] = (o * _silu(g_ref[0])).astype(o_ref.dtype)


def _attn_branch(QKV, P, tri2, T):
    B, S, _ = QKV.shape
    npairs = SB_WIDTH // LANES
    return pl.pallas_call(
        functools.partial(_attn_kernel, T=T),
        grid=(B, npairs, S // T),
        in_specs=[
            pl.BlockSpec((1, T, LANES), lambda b, p, i: (b, i, p)),
            pl.BlockSpec((1, S, LANES), lambda b, p, i: (b, 0, npairs + p)),
            pl.BlockSpec((1, S, LANES), lambda b, p, i: (b, 0, 2 * npairs + p)),
            pl.BlockSpec((1, T, LANES), lambda b, p, i: (b, i, COL_SG // LANES + p)),
            pl.BlockSpec((2 * T, T), lambda b, p, i: (0, 0)),
        ],
        out_specs=pl.BlockSpec((1, T, LANES), lambda b, p, i: (b, i, p)),
        out_shape=jax.ShapeDtypeStruct((B, S, SB_WIDTH), BF16),
        compiler_params=_cparams(("parallel", "parallel", "arbitrary")),
        name="sb_attention",
    )(QKV, QKV, QKV, P, tri2)


def _merge_kernel(x_ref, ys_ref, yp_ref, yb_ref, m0_ref, m1_ref, m2_ref, ws_ref, wp_ref, wb_ref,
                  wo_ref, fw_ref, o_ref, *, final_norm):
    merged = _sigmoid(m0_ref[...]) * jnp.dot(ys_ref[...], ws_ref[...], preferred_element_type=F32)
    merged = merged + _sigmoid(m1_ref[...]) * jnp.dot(yp_ref[...], wp_ref[...], preferred_element_type=F32)
    merged = merged + _sigmoid(m2_ref[...]) * jnp.dot(yb_ref[...], wb_ref[...], preferred_element_type=F32)
    out = x_ref[...] + jnp.dot(merged.astype(BF16), wo_ref[...], preferred_element_type=F32)
    if final_norm:
        var = jnp.mean(out * out, axis=-1, keepdims=True)
        out = (out * lax.rsqrt(var + EPS)) * fw_ref[...]
    o_ref[...] = out


def _merge(x2d, ys, yp, yb, P2d, ws, wp, wb, wo, fw, final_norm, tm):
    M, D = x2d.shape
    rows = lambda w, cb: pl.BlockSpec((tm, w), lambda i: (i, cb))
    const = lambda shape: pl.BlockSpec(shape, lambda i: (0, 0))
    m_blk = COL_M0 // D
    return pl.pallas_call(
        functools.partial(_merge_kernel, final_norm=final_norm),
        grid=(M // tm,),
        in_specs=[
            rows(D, 0), rows(SSM_WIDTH, 0), rows(POOL_WIDTH, 0), rows(SB_WIDTH, 0),
            rows(D, m_blk), rows(D, m_blk + 1), rows(D, m_blk + 2),
            const((SSM_WIDTH, D)), const((POOL_WIDTH, D)), const((SB_WIDTH, D)), const((D, D)),
            const((1, D)),
        ],
        out_specs=rows(D, 0),
        out_shape=jax.ShapeDtypeStruct((M, D), F32),
        compiler_params=_cparams(("parallel",)),
        name="merge_outproj",
    )(x2d, ys, yp, yb, P2d, P2d, P2d, ws, wp, wb, wo, fw)


def kernel(x, norm_w, w_in, conv_w, conv_b, dt_bias, a_log, d_skip, ssm_norm_w, pool_w, pool_scale,
           w_proj_ssm, w_proj_pool, w_proj_sb, w_out, final_norm_w):
    B, S, D = x.shape
    depth = w_in.shape[0]
    M = B * S
    attn_t = 256
    pool_t = 512

    c_xbc = SSM_WIDTH
    c_dt = c_xbc + SSM_WIDTH + BC_WIDTH
    c_pu = c_dt + SSM_HEADS
    c_qkv = c_pu + 2 * POOL_WIDTH
    c_sg = c_qkv + 3 * SB_WIDTH
    c_mg = c_sg + SB_WIDTH
    w_p = jnp.concatenate([w_in[:, :, 0:c_xbc + SSM_WIDTH], w_in[:, :, c_pu:c_qkv], w_in[:, :, c_sg:],
                           w_in[:, :, c_xbc + SSM_WIDTH:c_dt]], axis=2).astype(BF16)
    w_qkv = w_in[:, :, c_qkv:c_sg].astype(BF16)
    w_dt = jnp.pad(w_in[:, :, c_dt:c_pu], ((0, 0), (0, 0), (0, LANES - SSM_HEADS))).astype(BF16)
    pad_h = lambda a: jnp.pad(a, ((0, 0), (0, LANES - SSM_HEADS)))[:, None, :]
    dtb, alog = pad_h(dt_bias), pad_h(a_log)
    dskip_x = jnp.repeat(d_skip, SSM_HEAD_DIM, axis=1)[:, None, :]
    ws, wp, wb, wo = (w.astype(BF16) for w in (w_proj_ssm, w_proj_pool, w_proj_sb, w_out))
    pw = pool_w.astype(BF16)

    r = jnp.arange(CHUNK)
    tri = (r[None, :] <= r[:, None]).astype(BF16)
    hh = jnp.arange(LANES)[:, None]
    expand = ((jnp.arange(SSM_WIDTH)[None, :] // SSM_HEAD_DIM) == hh).astype(BF16)
    ra = jnp.arange(attn_t)
    triu = (ra[:, None] > ra[None, :]).astype(BF16)
    tri2 = jnp.concatenate([triu, triu], axis=0)

    x2d = x.reshape(M, D)
    for l in range(depth):
        P2d, dtraw = _inproj(x2d, norm_w[l][None, :], w_p[l], w_dt[l], F32, 1024, 1536)
        (QKV2d,) = _inproj(x2d, norm_w[l][None, :], w_qkv[l], None, BF16, 1024, 1536)
        P = P2d.reshape(B, S, P_COLS)
        QKV = QKV2d.reshape(B, S, QKV_COLS)
        ys = _ssm_branch(P, dtraw.reshape(B, S, LANES), conv_w[l], conv_b[l][None, :], dtb[l], alog[l],
                         dskip_x[l], ssm_norm_w[l][None, :], tri, expand)
        yp = _pool_branch(P, pw[l], pool_scale[l][None, :], pool_t)
        yb = _attn_branch(QKV, P, tri2, attn_t)
        x2d = _merge(x2d, ys.reshape(M, SSM_WIDTH), yp.reshape(M, POOL_WIDTH), yb.reshape(M, SB_WIDTH),
                     P2d, ws[l], wp[l], wb[l], wo[l], final_norm_w[None, :], l == depth - 1, 512)
    return x2d.reshape(B, S, D)
```

```python
import functools

import jax
import jax.numpy as jnp
from jax import lax
from jax.experimental import pallas as pl
from jax.experimental.pallas import tpu as pltpu

F32 = jnp.float32
BF16 = jnp.bfloat16

D_MODEL = 1024
SSM_WIDTH = 2048
SSM_HEAD_DIM = 64
SSM_HEADS = 32
SSM_GROUPS = 2
SSM_STATE = 128
CONV_WIDTH = 4
CHUNK = 128
BC_WIDTH = 2 * SSM_GROUPS * SSM_STATE
POOL_WIDTH = 1024
POOL_WINDOWS = (2, 4, 8, 16)
POOL_GROUP_DIM = 256
POOL_HALO = 16
SB_WIDTH = 1024
SB_HEAD_DIM = 64
SB_TILE = 256
SB_STEP_HEADS = 4
EPS = 1e-6
LANES = 128
SUBLANES = 8
CONV_HALO = 8

COL_Z, COL_XS, COL_PU, COL_PG, COL_SG, COL_M0, COL_BC = 0, 2048, 4096, 5120, 6144, 7168, 10240
P_COLS = 10752
QKV_COLS = 3072

VMEM_LIMIT = 56 * 1024 * 1024


def _cparams(sem):
    return pltpu.CompilerParams(dimension_semantics=sem, vmem_limit_bytes=VMEM_LIMIT)


def _sigmoid(x):
    return 1.0 / (1.0 + jnp.exp(-x))


def _silu(x):
    return x * _sigmoid(x)


def _softplus(x):
    return jnp.maximum(x, 0.0) + jnp.log1p(jnp.exp(-jnp.abs(x)))


def _split_bf16(x, n):
    parts, r = [], x
    for _ in range(n):
        p = r.astype(BF16)
        parts.append(p)
        r = r - p.astype(F32)
    return parts


def _rms_scale(x, w):
    var = jnp.mean(x * x, axis=-1, keepdims=True)
    return (x * lax.rsqrt(var + EPS)) * w


def _inproj_kernel(x_ref, nw_ref, w_ref, wdt_ref, o_ref, dt_ref, h_ref):
    @pl.when(pl.program_id(1) == 0)
    def _():
        hb = _rms_scale(x_ref[...], nw_ref[...]).astype(BF16)
        h_ref[...] = hb
        dt_ref[...] = jnp.dot(hb, wdt_ref[...], preferred_element_type=F32)

    o_ref[...] = jnp.dot(h_ref[...], w_ref[...], preferred_element_type=F32)


def _inproj(x2d, nw, w, wdt, tm, tn):
    M, D = x2d.shape
    N = w.shape[1]
    return pl.pallas_call(
        _inproj_kernel,
        grid=(M // tm, N // tn),
        in_specs=[
            pl.BlockSpec((tm, D), lambda i, j: (i, 0)),
            pl.BlockSpec((1, D), lambda i, j: (0, 0)),
            pl.BlockSpec((D, tn), lambda i, j: (0, j)),
            pl.BlockSpec((D, LANES), lambda i, j: (0, 0)),
        ],
        out_specs=[pl.BlockSpec((tm, tn), lambda i, j: (i, j)),
                   pl.BlockSpec((tm, LANES), lambda i, j: (i, 0))],
        out_shape=[jax.ShapeDtypeStruct((M, N), F32), jax.ShapeDtypeStruct((M, LANES), F32)],
        scratch_shapes=[pltpu.VMEM((tm, D), BF16)],
        compiler_params=_cparams(("parallel", "arbitrary")),
        name="inproj_main",
    )(x2d, nw, w, wdt)


def _inproj_qkv_kernel(x_ref, nw_ref, w_ref, o_ref, h_ref):
    @pl.when(pl.program_id(1) == 0)
    def _():
        h_ref[...] = _rms_scale(x_ref[...], nw_ref[...]).astype(BF16)

    o_ref[...] = jnp.dot(h_ref[...], w_ref[...], preferred_element_type=F32).astype(o_ref.dtype)


def _inproj_qkv(x2d, nw, w, tm, tn):
    M, D = x2d.shape
    return pl.pallas_call(
        _inproj_qkv_kernel,
        grid=(M // tm, QKV_COLS // tn),
        in_specs=[
            pl.BlockSpec((tm, D), lambda i, j: (i, 0)),
            pl.BlockSpec((1, D), lambda i, j: (0, 0)),
            pl.BlockSpec((D, tn), lambda i, j: (0, j)),
        ],
        out_specs=pl.BlockSpec((tm, tn), lambda i, j: (i, j)),
        out_shape=jax.ShapeDtypeStruct((M, QKV_COLS), BF16),
        scratch_shapes=[pltpu.VMEM((tm, D), BF16)],
        compiler_params=_cparams(("parallel", "arbitrary")),
        name="inproj_qkv",
    )(x2d, nw, w)


def _ssm_kernel(z_ref, xs_ref, bc_ref, dt_ref, cwx_ref, cbx_ref, cwbc_ref, cbbc_ref, dtb_ref,
                alog_ref, dskip_ref, nw_ref, tri_ref, exp_ref, o_ref,
                extx_ref, extbc_ref, state_ref):
    c = pl.program_id(1)
    L = CHUNK
    H0 = CONV_HALO

    @pl.when(c == 0)
    def _():
        extx_ref[0:H0, :] = jnp.zeros((H0, SSM_WIDTH), F32)
        extbc_ref[0:H0, :] = jnp.zeros((H0, BC_WIDTH), F32)
        state_ref[...] = jnp.zeros_like(state_ref)

    def conv(u_ref, ext_ref, w_ref, b_ref):
        ext_ref[H0:H0 + L, :] = u_ref[0]
        acc = b_ref[...] + w_ref[CONV_WIDTH - 1:CONV_WIDTH, :] * ext_ref[H0:H0 + L, :]
        for k in range(CONV_WIDTH - 1):
            s = CONV_WIDTH - 1 - k
            acc = acc + w_ref[k:k + 1, :] * ext_ref[H0 - s:H0 - s + L, :]
        ext_ref[0:H0, :] = ext_ref[L:L + H0, :]
        return _silu(acc)

    xs = conv(xs_ref, extx_ref, cwx_ref, cbx_ref)
    bc = conv(bc_ref, extbc_ref, cwbc_ref, cbbc_ref)

    dt = _softplus(dt_ref[0] + dtb_ref[...])
    a = -jnp.exp(alog_ref[...])
    adt = dt * a
    tri = tri_ref[...]
    a_cum = sum(jnp.dot(tri, p, preferred_element_type=F32) for p in _split_bf16(adt, 3))
    a_cum_t = a_cum.T
    a_last = a_cum[L - 1:L, :]

    stack = jnp.concatenate([dt, jnp.exp(a_last - a_cum), jnp.exp(a_cum)], axis=0)
    expanded = jnp.dot(jnp.concatenate(_split_bf16(stack, 2), axis=1), exp_ref[...],
                       preferred_element_type=F32)
    dt_x = expanded[0:L]
    dec_x = expanded[L:2 * L]
    expa_x = expanded[2 * L:3 * L]

    xdt = xs * dt_x
    dxs = (xdt * dec_x).astype(BF16)

    row = lax.broadcasted_iota(jnp.int32, (L, L), 0)
    col = lax.broadcasted_iota(jnp.int32, (L, L), 1)
    causal = row >= col
    lane = lax.broadcasted_iota(jnp.int32, (L, LANES), 1)
    first_half = lane < SSM_HEAD_DIM

    hpg = SSM_HEADS // SSM_GROUPS
    gw = hpg * SSM_HEAD_DIM
    y_parts = []
    for g in range(SSM_GROUPS):
        b_g = bc[:, g * SSM_STATE:(g + 1) * SSM_STATE]
        c_g = bc[:, (SSM_GROUPS + g) * SSM_STATE:(SSM_GROUPS + g + 1) * SSM_STATE].astype(BF16)
        cb = lax.dot_general(c_g, b_g.astype(BF16), (((1,), (1,)), ((), ())),
                             preferred_element_type=F32)
        for jj in range(hpg // 2):
            pj = g * (hpg // 2) + jj
            ms = []
            for h in (2 * pj, 2 * pj + 1):
                seg = a_cum[:, h:h + 1] - a_cum_t[h:h + 1, :]
                decay = jnp.where(causal, jnp.exp(jnp.where(causal, seg, 0.0)), 0.0)
                ms.append((cb * decay).astype(BF16))
            lhs = jnp.concatenate(ms, axis=1)
            xp = xdt[:, pj * LANES:(pj + 1) * LANES]
            rhs = jnp.concatenate([jnp.where(first_half, xp, 0.0).astype(BF16),
                                   jnp.where(first_half, 0.0, xp).astype(BF16)], axis=0)
            y_parts.append(jnp.dot(lhs, rhs, preferred_element_type=F32))
    y = jnp.concatenate(y_parts, axis=1)

    offs = []
    for g in range(SSM_GROUPS):
        b_g_t = bc[:, g * SSM_STATE:(g + 1) * SSM_STATE].T.astype(BF16)
        c_g = bc[:, (SSM_GROUPS + g) * SSM_STATE:(SSM_GROUPS + g + 1) * SSM_STATE].astype(BF16)
        prev = state_ref[g]
        offs.append(jnp.dot(c_g, prev.astype(BF16), preferred_element_type=F32))
        new = jnp.dot(b_g_t, dxs[:, g * gw:(g + 1) * gw], preferred_element_type=F32)
        state_ref[g] = prev * expa_x[L - 1:L, g * gw:(g + 1) * gw] + new
    y = y + jnp.concatenate(offs, axis=1) * expa_x + xs * dskip_ref[...]

    v = y * _silu(z_ref[0])
    o_ref[0] = _rms_scale(v, nw_ref[...]).astype(o_ref.dtype)


def _ssm_branch(P, dtraw, cw, cb, dtb, alog, dskip_x, nw, tri, expand2):
    B, S, _ = P.shape
    L = CHUNK
    cwx, cwbc = cw[:, :SSM_WIDTH], cw[:, SSM_WIDTH:]
    cbx, cbbc = cb[:, :SSM_WIDTH], cb[:, SSM_WIDTH:]
    const = lambda shape: pl.BlockSpec(shape, lambda b, c: (0,) * len(shape))
    return pl.pallas_call(
        _ssm_kernel,
        grid=(B, S // L),
        in_specs=[
            pl.BlockSpec((1, L, SSM_WIDTH), lambda b, c: (b, c, COL_Z // SSM_WIDTH)),
            pl.BlockSpec((1, L, SSM_WIDTH), lambda b, c: (b, c, COL_XS // SSM_WIDTH)),
            pl.BlockSpec((1, L, BC_WIDTH), lambda b, c: (b, c, COL_BC // BC_WIDTH)),
            pl.BlockSpec((1, L, LANES), lambda b, c: (b, c, 0)),
            const((CONV_WIDTH, SSM_WIDTH)), const((1, SSM_WIDTH)),
            const((CONV_WIDTH, BC_WIDTH)), const((1, BC_WIDTH)),
            const((1, LANES)), const((1, LANES)), const((1, SSM_WIDTH)), const((1, SSM_WIDTH)),
            const((L, L)), const((2 * LANES, SSM_WIDTH)),
        ],
        out_specs=pl.BlockSpec((1, L, SSM_WIDTH), lambda b, c: (b, c, 0)),
        out_shape=jax.ShapeDtypeStruct((B, S, SSM_WIDTH), BF16),
        scratch_shapes=[
            pltpu.VMEM((CONV_HALO + L, SSM_WIDTH), F32),
            pltpu.VMEM((CONV_HALO + L, BC_WIDTH), F32),
            pltpu.VMEM((SSM_GROUPS, SSM_STATE, SSM_WIDTH // SSM_GROUPS), F32),
        ],
        compiler_params=_cparams(("parallel", "arbitrary")),
        name="ssm_branch",
    )(P, P, P, dtraw, cwx, cbx, cwbc, cbbc, dtb, alog, dskip_x, nw, tri, expand2)


def _pool_kernel(u_ref, g_ref, w_ref, sc_ref, o_ref, ext_ref, *, T):
    t = pl.program_id(1)
    H0 = POOL_HALO

    @pl.when(t == 0)
    def _():
        ext_ref[0:H0, :] = jnp.zeros((H0, POOL_WIDTH), F32)

    ext_ref[H0:H0 + T, :] = u_ref[0]
    pos = t * T + lax.broadcasted_iota(jnp.int32, (T, 1), 0)
    outs = []
    for g, w in enumerate(POOL_WINDOWS):
        cs = slice(g * POOL_GROUP_DIM, (g + 1) * POOL_GROUP_DIM)
        u = ext_ref[H0:H0 + T, cs]
        win = u
        for k in range(1, w):
            win = win + ext_ref[H0 - k:H0 - k + T, cs]
        cnt = jnp.minimum(pos + 1, w).astype(F32)
        mixed = (win / cnt - u).astype(BF16)
        outs.append(jnp.dot(mixed, w_ref[g], preferred_element_type=F32))
    ext_ref[0:H0, :] = ext_ref[T:T + H0, :]
    mixed = jnp.concatenate(outs, axis=1)
    o_ref[0] = (mixed * sc_ref[...] * _silu(g_ref[0])).astype(o_ref.dtype)


def _pool_branch(P, pool_w, pool_scale, T):
    B, S, _ = P.shape
    return pl.pallas_call(
        functools.partial(_pool_kernel, T=T),
        grid=(B, S // T),
        in_specs=[
            pl.BlockSpec((1, T, POOL_WIDTH), lambda b, t: (b, t, COL_PU // POOL_WIDTH)),
            pl.BlockSpec((1, T, POOL_WIDTH), lambda b, t: (b, t, COL_PG // POOL_WIDTH)),
            pl.BlockSpec((len(POOL_WINDOWS), POOL_GROUP_DIM, POOL_GROUP_DIM), lambda b, t: (0, 0, 0)),
            pl.BlockSpec((1, POOL_WIDTH), lambda b, t: (0, 0)),
        ],
        out_specs=pl.BlockSpec((1, T, POOL_WIDTH), lambda b, t: (b, t, 0)),
        out_shape=jax.ShapeDtypeStruct((B, S, POOL_WIDTH), BF16),
        scratch_shapes=[pltpu.VMEM((POOL_HALO + T, POOL_WIDTH), F32)],
        compiler_params=_cparams(("parallel", "arbitrary")),
        name="pool_branch",
    )(P, P, pool_w, pool_scale)


def _neg_abs(x):
    sign = jnp.int32(-2 ** 31)
    return pltpu.bitcast(pltpu.bitcast(x, jnp.int32) | sign, F32)


def _attn_kernel(q_ref, k_ref, v_ref, g_ref, perm_ref, o_ref, kp_ref, vt_ref, *, T, S):
    i = pl.program_id(2)
    nv = T // SUBLANES
    nh = SB_STEP_HEADS
    hl = nh * SB_HEAD_DIM

    @pl.when(i == 0)
    def _():
        pm = perm_ref[...]
        for blk in range(S // T):
            rows = slice(blk * T, (blk + 1) * T)
            kp_ref[blk] = jnp.dot(pm, k_ref[0, rows, :], preferred_element_type=F32).astype(BF16)
            vp = jnp.dot(pm, v_ref[0, rows, :], preferred_element_type=F32).astype(BF16)
            vt_ref[blk] = vp.astype(F32).T.astype(BF16)

    lane = lax.broadcasted_iota(jnp.int32, (T, hl), 1)
    q = q_ref[0].astype(F32) * (SB_HEAD_DIM ** -0.5)
    q_heads = [jnp.where((lane >= h * SB_HEAD_DIM) & (lane < (h + 1) * SB_HEAD_DIM), q, 0.0).astype(BF16)
               for h in range(nh)]
    sub = lax.broadcasted_iota(jnp.int32, (SUBLANES, T), 0)
    q_idx = lax.broadcasted_iota(jnp.int32, (SUBLANES, T), 1)
    key_base = sub * nv

    def head_tile(zt, carry, diag):
        acc = jnp.ones((SUBLANES, T), F32)
        ws = [None] * nv
        for v in reversed(range(nv)):
            z = zt[v * SUBLANES:(v + 1) * SUBLANES, :]
            e = jnp.exp(_neg_abs(z))
            r = 1.0 / (1.0 + e)
            er = e * r
            pos = z >= 0.0
            beta = jnp.where(pos, r, er)
            omb = jnp.where(pos, er, r)
            if diag:
                causal = key_base + v < q_idx
                beta = jnp.where(causal, beta, 0.0)
                omb = jnp.where(causal, omb, 1.0)
            ws[v] = beta * acc
            acc = acc * omb
        exc = jnp.where(sub + 1 < SUBLANES, pltpu.roll(acc, SUBLANES - 1, axis=0), 1.0)
        for sh in (1, 2, 4):
            exc = exc * jnp.where(sub + sh < SUBLANES, pltpu.roll(exc, SUBLANES - sh, axis=0), 1.0)
        offb = exc * carry
        att_t = jnp.concatenate([ws[v] * offb for v in range(nv)], axis=0).astype(BF16)
        return att_t, carry * (acc[0:1, :] * exc[0:1, :])

    def tile(j, carries, accs, diag):
        kp = kp_ref[j]
        zts = [lax.dot_general(kp, qh, (((1,), (1,)), ((), ())), preferred_element_type=F32)
               for qh in q_heads]
        new_c, new_a = [], []
        for h in range(nh):
            att_t, c = head_tile(zts[h], carries[h], diag)
            vt_h = vt_ref[j, h * SB_HEAD_DIM:(h + 1) * SB_HEAD_DIM, :]
            new_a.append(accs[h] + jnp.dot(vt_h, att_t, preferred_element_type=F32))
            new_c.append(c)
        return tuple(new_c), tuple(new_a)

    def live(carries):
        m = carries[0]
        for c in carries[1:]:
            m = jnp.maximum(m, c)
        return (jnp.max(m) > 0.0).astype(jnp.int32)

    carries = tuple(jnp.ones((1, T), F32) for _ in range(nh))
    accs = tuple(jnp.zeros((SB_HEAD_DIM, T), F32) for _ in range(nh))
    carries, accs = tile(i, carries, accs, True)

    def cond(st):
        return jnp.logical_and(st[0] >= 0, st[3] > 0)

    def body(st):
        c, a = tile(st[0], st[1], st[2], False)
        return st[0] - 1, c, a, live(c)

    _, carries, accs, _ = lax.while_loop(cond, body, (i - 1, carries, accs, live(carries)))
    o = jnp.concatenate(accs, axis=0).T
    o_ref[0] = (o * _silu(g_ref[0])).astype(o_ref.dtype)


def _attn_branch(QKV, P, perm):
    B, S, _ = QKV.shape
    T = SB_TILE
    hl = SB_STEP_HEADS * SB_HEAD_DIM
    nblk = SB_WIDTH // hl
    return pl.pallas_call(
        functools.partial(_attn_kernel, T=T, S=S),
        grid=(B, nblk, S // T),
        in_specs=[
            pl.BlockSpec((1, T, hl), lambda b, p, i: (b, i, p)),
            pl.BlockSpec((1, S, hl), lambda b, p, i: (b, 0, nblk + p)),
            pl.BlockSpec((1, S, hl), lambda b, p, i: (b, 0, 2 * nblk + p)),
            pl.BlockSpec((1, T, hl), lambda b, p, i: (b, i, COL_SG // hl + p)),
            pl.BlockSpec((T, T), lambda b, p, i: (0, 0)),
        ],
        out_specs=pl.BlockSpec((1, T, hl), lambda b, p, i: (b, i, p)),
        out_shape=jax.ShapeDtypeStruct((B, S, SB_WIDTH), BF16),
        scratch_shapes=[pltpu.VMEM((S // T, T, hl), BF16), pltpu.VMEM((S // T, hl, T), BF16)],
        compiler_params=_cparams(("parallel", "parallel", "arbitrary")),
        name="sb_attention",
    )(QKV, QKV, QKV, P, perm)


def _merge_kernel(x_ref, ys_ref, yp_ref, yb_ref, m0_ref, m1_ref, m2_ref, ws_ref, wp_ref, wb_ref,
                  wo_ref, fw_ref, o_ref, *, final_norm):
    merged = _sigmoid(m0_ref[...]) * jnp.dot(ys_ref[...], ws_ref[...], preferred_element_type=F32)
    merged = merged + _sigmoid(m1_ref[...]) * jnp.dot(yp_ref[...], wp_ref[...], preferred_element_type=F32)
    merged = merged + _sigmoid(m2_ref[...]) * jnp.dot(yb_ref[...], wb_ref[...], preferred_element_type=F32)
    out = x_ref[...] + jnp.dot(merged.astype(BF16), wo_ref[...], preferred_element_type=F32)
    if final_norm:
        out = _rms_scale(out, fw_ref[...])
    o_ref[...] = out


def _merge(x2d, ys, yp, yb, P2d, ws, wp, wb, wo, fw, final_norm, tm):
    M, D = x2d.shape
    rows = lambda w, cb: pl.BlockSpec((tm, w), lambda i: (i, cb))
    const = lambda shape: pl.BlockSpec(shape, lambda i: (0, 0))
    m_blk = COL_M0 // D
    return pl.pallas_call(
        functools.partial(_merge_kernel, final_norm=final_norm),
        grid=(M // tm,),
        in_specs=[
            rows(D, 0), rows(SSM_WIDTH, 0), rows(POOL_WIDTH, 0), rows(SB_WIDTH, 0),
            rows(D, m_blk), rows(D, m_blk + 1), rows(D, m_blk + 2),
            const((SSM_WIDTH, D)), const((POOL_WIDTH, D)), const((SB_WIDTH, D)), const((D, D)),
            const((1, D)),
        ],
        out_specs=rows(D, 0),
        out_shape=jax.ShapeDtypeStruct((M, D), F32),
        compiler_params=_cparams(("parallel",)),
        name="merge_outproj",
    )(x2d, ys, yp, yb, P2d, P2d, P2d, ws, wp, wb, wo, fw)


def kernel(x, norm_w, w_in, conv_w, conv_b, dt_bias, a_log, d_skip, ssm_norm_w, pool_w, pool_scale,
           w_proj_ssm, w_proj_pool, w_proj_sb, w_out, final_norm_w):
    B, S, D = x.shape
    depth = w_in.shape[0]
    M = B * S
    pool_t = 512

    c_xbc = SSM_WIDTH
    c_dt = c_xbc + SSM_WIDTH + BC_WIDTH
    c_pu = c_dt + SSM_HEADS
    c_qkv = c_pu + 2 * POOL_WIDTH
    c_sg = c_qkv + 3 * SB_WIDTH
    w_p = jnp.concatenate([w_in[:, :, 0:c_xbc + SSM_WIDTH], w_in[:, :, c_pu:c_qkv], w_in[:, :, c_sg:],
                           w_in[:, :, c_xbc + SSM_WIDTH:c_dt]], axis=2).astype(BF16)
    w_qkv = w_in[:, :, c_qkv:c_sg].astype(BF16)
    w_dt = jnp.pad(w_in[:, :, c_dt:c_pu], ((0, 0), (0, 0), (0, LANES - SSM_HEADS))).astype(BF16)
    pad_h = lambda a: jnp.pad(a, ((0, 0), (0, LANES - SSM_HEADS)))[:, None, :]
    dtb, alog = pad_h(dt_bias), pad_h(a_log)
    dskip_x = jnp.repeat(d_skip, SSM_HEAD_DIM, axis=1)[:, None, :]
    ws, wp, wb, wo = (w.astype(BF16) for w in (w_proj_ssm, w_proj_pool, w_proj_sb, w_out))
    pw = pool_w.astype(BF16)

    r = jnp.arange(CHUNK)
    tri = (r[None, :] <= r[:, None]).astype(BF16)
    hh = jnp.arange(LANES)[:, None]
    expand = ((jnp.arange(SSM_WIDTH)[None, :] // SSM_HEAD_DIM) == hh).astype(BF16)
    expand2 = jnp.concatenate([expand, expand], axis=0)
    rr = jnp.arange(SB_TILE)
    strip_key = (rr % SUBLANES) * (SB_TILE // SUBLANES) + rr // SUBLANES
    perm = (strip_key[:, None] == rr[None, :]).astype(BF16)

    x2d = x.reshape(M, D)
    for l in range(depth):
        P2d, dtraw = _inproj(x2d, norm_w[l][None, :], w_p[l], w_dt[l], 1024, 1536)
        QKV2d = _inproj_qkv(x2d, norm_w[l][None, :], w_qkv[l], 1024, 1536)
        P = P2d.reshape(B, S, P_COLS)
        QKV = QKV2d.reshape(B, S, QKV_COLS)
        ys = _ssm_branch(P, dtraw.reshape(B, S, LANES), conv_w[l], conv_b[l][None, :], dtb[l], alog[l],
                         dskip_x[l], ssm_norm_w[l][None, :], tri, expand2)
        yp = _pool_branch(P, pw[l], pool_scale[l][None, :], pool_t)
        yb = _attn_branch(QKV, P, perm)
        x2d = _merge(x2d, ys.reshape(M, SSM_WIDTH), yp.reshape(M, POOL_WIDTH), yb.reshape(M, SB_WIDTH),
                     P2d, ws[l], wp[l], wb[l], wo[l], final_norm_w[None, :], l == depth - 1, 512)
    return x2d.reshape(B, S, D)
```

```python
import functools

import jax
import jax.numpy as jnp
from jax import lax
from jax.experimental import pallas as pl
from jax.experimental.pallas import tpu as pltpu

F32 = jnp.float32
BF16 = jnp.bfloat16

D_MODEL = 1024
SSM_WIDTH = 2048
SSM_HEAD_DIM = 64
SSM_HEADS = 32
SSM_GROUPS = 2
SSM_STATE = 128
CONV_WIDTH = 4
CHUNK = 128
BC_WIDTH = 2 * SSM_GROUPS * SSM_STATE
POOL_WIDTH = 1024
POOL_WINDOWS = (2, 4, 8, 16)
POOL_GROUP_DIM = 256
POOL_HALO = 16
SB_WIDTH = 1024
SB_HEAD_DIM = 64
SB_TILE = 256
SB_STEP_HEADS = 16
EPS = 1e-6
LANES = 128
SUBLANES = 8
CONV_HALO = 8

COL_Z, COL_XS, COL_PU, COL_PG, COL_SG, COL_M0 = 0, 2048, 4096, 5120, 6144, 7168
COL_Q, COL_K, COL_V, COL_BC = 10240, 11264, 12288, 13312
P_COLS = 13824

VMEM_LIMIT = 56 * 1024 * 1024


def _cparams(sem):
    return pltpu.CompilerParams(dimension_semantics=sem, vmem_limit_bytes=VMEM_LIMIT)


def _sigmoid(x):
    return 0.5 + 0.5 * jnp.tanh(0.5 * x)


def _silu(x):
    hx = 0.5 * x
    return hx + hx * jnp.tanh(hx)


def _softplus(x):
    return jnp.maximum(x, 0.0) + jnp.log1p(jnp.exp(-jnp.abs(x)))


def _split_bf16(x, n):
    parts, r = [], x
    for _ in range(n):
        p = r.astype(BF16)
        parts.append(p)
        r = r - p.astype(F32)
    return parts


def _rms_scale(x, w):
    var = jnp.mean(x * x, axis=-1, keepdims=True)
    return (x * lax.rsqrt(var + EPS)) * w


def _inproj_kernel(x_ref, nw_ref, w_ref, wdt_ref, o_ref, dt_ref, h_ref):
    @pl.when(pl.program_id(1) == 0)
    def _():
        hb = _rms_scale(x_ref[...], nw_ref[...]).astype(BF16)
        h_ref[...] = hb
        dt_ref[...] = jnp.dot(hb, wdt_ref[...], preferred_element_type=F32)

    o_ref[...] = jnp.dot(h_ref[...], w_ref[...], preferred_element_type=F32).astype(o_ref.dtype)


def _inproj(x2d, nw, w, wdt, tm, tn):
    M, D = x2d.shape
    N = w.shape[1]
    return pl.pallas_call(
        _inproj_kernel,
        grid=(M // tm, N // tn),
        in_specs=[
            pl.BlockSpec((tm, D), lambda i, j: (i, 0)),
            pl.BlockSpec((1, D), lambda i, j: (0, 0)),
            pl.BlockSpec((D, tn), lambda i, j: (0, j)),
            pl.BlockSpec((D, LANES), lambda i, j: (0, 0)),
        ],
        out_specs=[pl.BlockSpec((tm, tn), lambda i, j: (i, j)),
                   pl.BlockSpec((tm, LANES), lambda i, j: (i, 0))],
        out_shape=[jax.ShapeDtypeStruct((M, N), BF16), jax.ShapeDtypeStruct((M, LANES), F32)],
        scratch_shapes=[pltpu.VMEM((tm, D), BF16)],
        compiler_params=_cparams(("parallel", "arbitrary")),
        name="inproj",
    )(x2d, nw, w, wdt)


def _ssm_kernel(z_ref, xs_ref, bc_ref, dt_ref, cwx_ref, cbx_ref, cwbc_ref, cbbc_ref, dtb_ref,
                alog_ref, dskip_ref, nw_ref, tri_ref, exp_ref, o_ref,
                extx_ref, extbc_ref, state_ref):
    c = pl.program_id(1)
    L = CHUNK
    H0 = CONV_HALO

    @pl.when(c == 0)
    def _():
        extx_ref[0:H0, :] = jnp.zeros((H0, SSM_WIDTH), F32)
        extbc_ref[0:H0, :] = jnp.zeros((H0, BC_WIDTH), F32)
        state_ref[...] = jnp.zeros_like(state_ref)

    def conv(u_ref, ext_ref, w_ref, b_ref):
        ext_ref[H0:H0 + L, :] = u_ref[0].astype(F32)
        acc = b_ref[...] + w_ref[CONV_WIDTH - 1:CONV_WIDTH, :] * ext_ref[H0:H0 + L, :]
        for k in range(CONV_WIDTH - 1):
            s = CONV_WIDTH - 1 - k
            acc = acc + w_ref[k:k + 1, :] * ext_ref[H0 - s:H0 - s + L, :]
        ext_ref[0:H0, :] = ext_ref[L:L + H0, :]
        return _silu(acc)

    xs = conv(xs_ref, extx_ref, cwx_ref, cbx_ref)
    bc = conv(bc_ref, extbc_ref, cwbc_ref, cbbc_ref)

    dt = _softplus(dt_ref[0] + dtb_ref[...])
    a = -jnp.exp(alog_ref[...])
    adt = dt * a
    tri = tri_ref[...]
    a_cum = sum(jnp.dot(tri, p, preferred_element_type=F32) for p in _split_bf16(adt, 3))
    a_cum_t = a_cum.T
    a_last = a_cum[L - 1:L, :]

    stack = jnp.concatenate([dt, jnp.exp(a_last - a_cum), jnp.exp(a_cum)], axis=0)
    expanded = jnp.dot(jnp.concatenate(_split_bf16(stack, 2), axis=1), exp_ref[...],
                       preferred_element_type=F32)
    dt_x = expanded[0:L]
    dec_x = expanded[L:2 * L]
    expa_x = expanded[2 * L:3 * L]

    xdt = xs * dt_x
    dxs = (xdt * dec_x).astype(BF16)

    row = lax.broadcasted_iota(jnp.int32, (L, L), 0)
    col = lax.broadcasted_iota(jnp.int32, (L, L), 1)
    causal = row >= col
    lane = lax.broadcasted_iota(jnp.int32, (L, LANES), 1)
    first_half = lane < SSM_HEAD_DIM

    hpg = SSM_HEADS // SSM_GROUPS
    gw = hpg * SSM_HEAD_DIM
    y_parts = []
    for g in range(SSM_GROUPS):
        b_g = bc[:, g * SSM_STATE:(g + 1) * SSM_STATE]
        c_g = bc[:, (SSM_GROUPS + g) * SSM_STATE:(SSM_GROUPS + g + 1) * SSM_STATE].astype(BF16)
        cb = lax.dot_general(c_g, b_g.astype(BF16), (((1,), (1,)), ((), ())),
                             preferred_element_type=F32)
        cb = jnp.where(causal, cb, 0.0)
        for jj in range(hpg // 2):
            pj = g * (hpg // 2) + jj
            ms = []
            for h in (2 * pj, 2 * pj + 1):
                seg = jnp.minimum(a_cum[:, h:h + 1] - a_cum_t[h:h + 1, :], 0.0)
                ms.append((cb * jnp.exp(seg)).astype(BF16))
            lhs = jnp.concatenate(ms, axis=1)
            xp = xdt[:, pj * LANES:(pj + 1) * LANES]
            rhs = jnp.concatenate([jnp.where(first_half, xp, 0.0).astype(BF16),
                                   jnp.where(first_half, 0.0, xp).astype(BF16)], axis=0)
            y_parts.append(jnp.dot(lhs, rhs, preferred_element_type=F32))
    y = jnp.concatenate(y_parts, axis=1)

    offs = []
    for g in range(SSM_GROUPS):
        b_g_t = bc[:, g * SSM_STATE:(g + 1) * SSM_STATE].T.astype(BF16)
        c_g = bc[:, (SSM_GROUPS + g) * SSM_STATE:(SSM_GROUPS + g + 1) * SSM_STATE].astype(BF16)
        prev = state_ref[g]
        offs.append(jnp.dot(c_g, prev.astype(BF16), preferred_element_type=F32))
        new = jnp.dot(b_g_t, dxs[:, g * gw:(g + 1) * gw], preferred_element_type=F32)
        state_ref[g] = prev * expa_x[L - 1:L, g * gw:(g + 1) * gw] + new
    y = y + jnp.concatenate(offs, axis=1) * expa_x + xs * dskip_ref[...]

    v = y * _silu(z_ref[0].astype(F32))
    o_ref[0] = _rms_scale(v, nw_ref[...]).astype(o_ref.dtype)


def _ssm_branch(P, dtraw, cw, cb, dtb, alog, dskip_x, nw, tri, expand2):
    B, S, _ = P.shape
    L = CHUNK
    cwx, cwbc = cw[:, :SSM_WIDTH], cw[:, SSM_WIDTH:]
    cbx, cbbc = cb[:, :SSM_WIDTH], cb[:, SSM_WIDTH:]
    const = lambda shape: pl.BlockSpec(shape, lambda b, c: (0,) * len(shape))
    return pl.pallas_call(
        _ssm_kernel,
        grid=(B, S // L),
        in_specs=[
            pl.BlockSpec((1, L, SSM_WIDTH), lambda b, c: (b, c, COL_Z // SSM_WIDTH)),
            pl.BlockSpec((1, L, SSM_WIDTH), lambda b, c: (b, c, COL_XS // SSM_WIDTH)),
            pl.BlockSpec((1, L, BC_WIDTH), lambda b, c: (b, c, COL_BC // BC_WIDTH)),
            pl.BlockSpec((1, L, LANES), lambda b, c: (b, c, 0)),
            const((CONV_WIDTH, SSM_WIDTH)), const((1, SSM_WIDTH)),
            const((CONV_WIDTH, BC_WIDTH)), const((1, BC_WIDTH)),
            const((1, LANES)), const((1, LANES)), const((1, SSM_WIDTH)), const((1, SSM_WIDTH)),
            const((L, L)), const((2 * LANES, SSM_WIDTH)),
        ],
        out_specs=pl.BlockSpec((1, L, SSM_WIDTH), lambda b, c: (b, c, 0)),
        out_shape=jax.ShapeDtypeStruct((B, S, SSM_WIDTH), BF16),
        scratch_shapes=[
            pltpu.VMEM((CONV_HALO + L, SSM_WIDTH), F32),
            pltpu.VMEM((CONV_HALO + L, BC_WIDTH), F32),
            pltpu.VMEM((SSM_GROUPS, SSM_STATE, SSM_WIDTH // SSM_GROUPS), F32),
        ],
        compiler_params=_cparams(("parallel", "arbitrary")),
        name="ssm_branch",
    )(P, P, P, dtraw, cwx, cbx, cwbc, cbbc, dtb, alog, dskip_x, nw, tri, expand2)


def _pool_kernel(u_ref, g_ref, w_ref, sc_ref, o_ref, ext_ref, *, T):
    t = pl.program_id(1)
    H0 = POOL_HALO

    @pl.when(t == 0)
    def _():
        ext_ref[0:H0, :] = jnp.zeros((H0, POOL_WIDTH), F32)

    ext_ref[H0:H0 + T, :] = u_ref[0].astype(F32)
    pos = t * T + lax.broadcasted_iota(jnp.int32, (T, 1), 0)
    outs = []
    for g, w in enumerate(POOL_WINDOWS):
        cs = slice(g * POOL_GROUP_DIM, (g + 1) * POOL_GROUP_DIM)
        u = ext_ref[H0:H0 + T, cs]
        win = u
        for k in range(1, w):
            win = win + ext_ref[H0 - k:H0 - k + T, cs]
        cnt = jnp.minimum(pos + 1, w).astype(F32)
        mixed = (win / cnt - u).astype(BF16)
        outs.append(jnp.dot(mixed, w_ref[g], preferred_element_type=F32))
    ext_ref[0:H0, :] = ext_ref[T:T + H0, :]
    mixed = jnp.concatenate(outs, axis=1)
    o_ref[0] = (mixed * sc_ref[...] * _silu(g_ref[0].astype(F32))).astype(o_ref.dtype)


def _pool_branch(P, pool_w, pool_scale, T):
    B, S, _ = P.shape
    return pl.pallas_call(
        functools.partial(_pool_kernel, T=T),
        grid=(B, S // T),
        in_specs=[
            pl.BlockSpec((1, T, POOL_WIDTH), lambda b, t: (b, t, COL_PU // POOL_WIDTH)),
            pl.BlockSpec((1, T, POOL_WIDTH), lambda b, t: (b, t, COL_PG // POOL_WIDTH)),
            pl.BlockSpec((len(POOL_WINDOWS), POOL_GROUP_DIM, POOL_GROUP_DIM), lambda b, t: (0, 0, 0)),
            pl.BlockSpec((1, POOL_WIDTH), lambda b, t: (0, 0)),
        ],
        out_specs=pl.BlockSpec((1, T, POOL_WIDTH), lambda b, t: (b, t, 0)),
        out_shape=jax.ShapeDtypeStruct((B, S, POOL_WIDTH), BF16),
        scratch_shapes=[pltpu.VMEM((POOL_HALO + T, POOL_WIDTH), F32)],
        compiler_params=_cparams(("parallel", "arbitrary")),
        name="pool_branch",
    )(P, P, pool_w, pool_scale)


def _attn_kernel(q_ref, k_ref, v_ref, g_ref, perm_ref, o_ref, kp_ref, vt_ref, acc_ref, *, T, S):
    i = pl.program_id(2)
    nv = T // SUBLANES
    nh = SB_STEP_HEADS
    hl = nh * SB_HEAD_DIM

    @pl.when(i == 0)
    def _():
        pm = perm_ref[...]
        for blk in range(S // T):
            rows = slice(blk * T, (blk + 1) * T)
            kp_ref[blk] = jnp.dot(pm, k_ref[0, rows, :], preferred_element_type=F32).astype(BF16)
            vp = jnp.dot(pm, v_ref[0, rows, :], preferred_element_type=F32).astype(BF16)
            vt_ref[blk] = vp.astype(F32).T.astype(BF16)

    first = lax.broadcasted_iota(jnp.int32, (T, LANES), 1) < SB_HEAD_DIM
    q = q_ref[0].astype(F32) * (0.5 * SB_HEAD_DIM ** -0.5)
    q_heads = []
    for h in range(nh):
        qp = q[:, (h // 2) * LANES:(h // 2 + 1) * LANES]
        q_heads.append((jnp.where(first, qp, 0.0) if h % 2 == 0 else jnp.where(first, 0.0, qp)).astype(BF16))
    sub = lax.broadcasted_iota(jnp.int32, (SUBLANES, T), 0)
    q_idx = lax.broadcasted_iota(jnp.int32, (SUBLANES, T), 1)
    key_base = sub * nv

    def head_tile(zt, carry, diag):
        acc = jnp.ones((SUBLANES, T), F32)
        ws = [None] * nv
        for v in reversed(range(nv)):
            ht = 0.5 * jnp.tanh(zt[v * SUBLANES:(v + 1) * SUBLANES, :])
            beta = 0.5 + ht
            omb = 0.5 - ht
            if diag:
                causal = key_base + v < q_idx
                beta = jnp.where(causal, beta, 0.0)
                omb = jnp.where(causal, omb, 1.0)
            ws[v] = beta * acc
            acc = acc * omb
        exc = jnp.where(sub + 1 < SUBLANES, pltpu.roll(acc, SUBLANES - 1, axis=0), 1.0)
        for sh in (1, 2, 4):
            exc = exc * jnp.where(sub + sh < SUBLANES, pltpu.roll(exc, SUBLANES - sh, axis=0), 1.0)
        offb = exc * carry
        att_t = jnp.concatenate([ws[v] * offb for v in range(nv)], axis=0).astype(BF16)
        return att_t, carry * (acc[0:1, :] * exc[0:1, :])

    def tile(j, carries, diag):
        kp = kp_ref[j]
        zts = [lax.dot_general(kp[:, (h // 2) * LANES:(h // 2 + 1) * LANES], q_heads[h],
                               (((1,), (1,)), ((), ())), preferred_element_type=F32) for h in range(nh)]
        new_c = []
        for h in range(nh):
            att_t, c = head_tile(zts[h], carries[h], diag)
            rows = slice(h * SB_HEAD_DIM, (h + 1) * SB_HEAD_DIM)
            upd = jnp.dot(vt_ref[j, rows, :], att_t, preferred_element_type=F32)
            acc_ref[rows, :] = upd if diag else acc_ref[rows, :] + upd
            new_c.append(c)
        return tuple(new_c)

    def live(carries):
        m = carries[0]
        for c in carries[1:]:
            m = jnp.maximum(m, c)
        return (jnp.max(m) > 0.0).astype(jnp.int32)

    carries = tile(i, tuple(jnp.ones((1, T), F32) for _ in range(nh)), True)

    def cond(st):
        return jnp.logical_and(st[0] >= 0, st[2] > 0)

    def body(st):
        c = tile(st[0], st[1], False)
        return st[0] - 1, c, live(c)

    lax.while_loop(cond, body, (i - 1, carries, live(carries)))
    o = acc_ref[...].T
    o_ref[0] = (o * _silu(g_ref[0].astype(F32))).astype(o_ref.dtype)


def _attn_branch(P, perm):
    B, S, _ = P.shape
    T = SB_TILE
    hl = SB_STEP_HEADS * SB_HEAD_DIM
    nblk = SB_WIDTH // hl
    return pl.pallas_call(
        functools.partial(_attn_kernel, T=T, S=S),
        grid=(B, nblk, S // T),
        in_specs=[
            pl.BlockSpec((1, T, hl), lambda b, p, i: (b, i, COL_Q // hl + p)),
            pl.BlockSpec((1, S, hl), lambda b, p, i: (b, 0, COL_K // hl + p)),
            pl.BlockSpec((1, S, hl), lambda b, p, i: (b, 0, COL_V // hl + p)),
            pl.BlockSpec((1, T, hl), lambda b, p, i: (b, i, COL_SG // hl + p)),
            pl.BlockSpec((T, T), lambda b, p, i: (0, 0)),
        ],
        out_specs=pl.BlockSpec((1, T, hl), lambda b, p, i: (b, i, p)),
        out_shape=jax.ShapeDtypeStruct((B, S, SB_WIDTH), BF16),
        scratch_shapes=[pltpu.VMEM((S // T, T, hl), BF16), pltpu.VMEM((S // T, hl, T), BF16),
                        pltpu.VMEM((hl, T), F32)],
        compiler_params=_cparams(("parallel", "parallel", "arbitrary")),
        name="sb_attention",
    )(P, P, P, P, perm)


def _merge_kernel(x_ref, ys_ref, yp_ref, yb_ref, m0_ref, m1_ref, m2_ref, ws_ref, wp_ref, wb_ref,
                  wo_ref, fw_ref, o_ref, *, final_norm):
    gate = lambda m_ref: _sigmoid(m_ref[...].astype(F32))
    merged = gate(m0_ref) * jnp.dot(ys_ref[...], ws_ref[...], preferred_element_type=F32)
    merged = merged + gate(m1_ref) * jnp.dot(yp_ref[...], wp_ref[...], preferred_element_type=F32)
    merged = merged + gate(m2_ref) * jnp.dot(yb_ref[...], wb_ref[...], preferred_element_type=F32)
    out = x_ref[...] + jnp.dot(merged.astype(BF16), wo_ref[...], preferred_element_type=F32)
    if final_norm:
        out = _rms_scale(out, fw_ref[...])
    o_ref[...] = out


def _merge(x2d, ys, yp, yb, P2d, ws, wp, wb, wo, fw, final_norm, tm):
    M, D = x2d.shape
    rows = lambda w, cb: pl.BlockSpec((tm, w), lambda i: (i, cb))
    const = lambda shape: pl.BlockSpec(shape, lambda i: (0, 0))
    m_blk = COL_M0 // D
    return pl.pallas_call(
        functools.partial(_merge_kernel, final_norm=final_norm),
        grid=(M // tm,),
        in_specs=[
            rows(D, 0), rows(SSM_WIDTH, 0), rows(POOL_WIDTH, 0), rows(SB_WIDTH, 0),
            rows(D, m_blk), rows(D, m_blk + 1), rows(D, m_blk + 2),
            const((SSM_WIDTH, D)), const((POOL_WIDTH, D)), const((SB_WIDTH, D)), const((D, D)),
            const((1, D)),
        ],
        out_specs=rows(D, 0),
        out_shape=jax.ShapeDtypeStruct((M, D), F32),
        compiler_params=_cparams(("parallel",)),
        name="merge_outproj",
    )(x2d, ys, yp, yb, P2d, P2d, P2d, ws, wp, wb, wo, fw)


def kernel(x, norm_w, w_in, conv_w, conv_b, dt_bias, a_log, d_skip, ssm_norm_w, pool_w, pool_scale,
           w_proj_ssm, w_proj_pool, w_proj_sb, w_out, final_norm_w):
    B, S, D = x.shape
    depth = w_in.shape[0]
    M = B * S
    pool_t = 512

    c_xbc = SSM_WIDTH
    c_dt = c_xbc + SSM_WIDTH + BC_WIDTH
    c_pu = c_dt + SSM_HEADS
    c_qkv = c_pu + 2 * POOL_WIDTH
    c_sg = c_qkv + 3 * SB_WIDTH
    w_p = jnp.concatenate([w_in[:, :, 0:c_xbc + SSM_WIDTH], w_in[:, :, c_pu:c_qkv], w_in[:, :, c_sg:],
                           w_in[:, :, c_qkv:c_sg], w_in[:, :, c_xbc + SSM_WIDTH:c_dt]], axis=2).astype(BF16)
    w_dt =jnp.pad(w_in[:, :, c_dt:c_pu], ((0, 0), (0, 0), (0, LANES - SSM_HEADS))).astype(BF16)
    pad_h = lambda a: jnp.pad(a, ((0, 0), (0, LANES - SSM_HEADS)))[:, None, :]
    dtb, alog = pad_h(dt_bias), pad_h(a_log)
    dskip_x = jnp.repeat(d_skip, SSM_HEAD_DIM, axis=1)[:, None, :]
    ws, wp, wb, wo = (w.astype(BF16) for w in (w_proj_ssm, w_proj_pool, w_proj_sb, w_out))
    pw = pool_w.astype(BF16)

    r = jnp.arange(CHUNK)
    tri = (r[None, :] <= r[:, None]).astype(BF16)
    hh = jnp.arange(LANES)[:, None]
    expand = ((jnp.arange(SSM_WIDTH)[None, :] // SSM_HEAD_DIM) == hh).astype(BF16)
    expand2 = jnp.concatenate([expand, expand], axis=0)
    rr = jnp.arange(SB_TILE)
    strip_key = (rr % SUBLANES) * (SB_TILE // SUBLANES) + rr // SUBLANES
    perm = (strip_key[:, None] == rr[None, :]).astype(BF16)

    x2d = x.reshape(M, D)
    for l in range(depth):
        P2d, dtraw = _inproj(x2d, norm_w[l][None, :], w_p[l], w_dt[l], min(2048, M), 1536)
        P = P2d.reshape(B, S, P_COLS)
        ys =_ssm_branch(P, dtraw.reshape(B, S, LANES), conv_w[l], conv_b[l][None, :], dtb[l], alog[l],
                         dskip_x[l], ssm_norm_w[l][None, :], tri, expand2)
        yp = _pool_branch(P, pw[l], pool_scale[l][None, :], pool_t)
        yb = _attn_branch(P, perm)
        x2d = _merge(x2d, ys.reshape(M, SSM_WIDTH), yp.reshape(M, POOL_WIDTH), yb.reshape(M, SB_WIDTH),
                     P2d, ws[l], wp[l], wb[l], wo[l], final_norm_w[None, :], l == depth - 1, 512)
    return x2d.reshape(B, S, D)
```

```python
import functools

import jax
import jax.numpy as jnp
from jax import lax
from jax.experimental import pallas as pl
from jax.experimental.pallas import tpu as pltpu

F32 = jnp.float32
BF16 = jnp.bfloat16

D_MODEL = 1024
SSM_WIDTH = 2048
SSM_HEAD_DIM = 64
SSM_HEADS = 32
SSM_GROUPS = 2
SSM_STATE = 128
CONV_WIDTH = 4
CHUNK = 128
BC_WIDTH = 2 * SSM_GROUPS * SSM_STATE
POOL_WIDTH = 1024
POOL_WINDOWS = (2, 4, 8, 16)
POOL_GROUP_DIM = 256
POOL_HALO = 16
SB_WIDTH = 1024
SB_HEAD_DIM = 64
SB_TILE = 256
SB_STEP_HEADS = 16
EPS = 1e-6
LANES = 128
SUBLANES = 8
CONV_HALO = 8

COL_Z, COL_XS, COL_PU, COL_PG, COL_SG, COL_M0 = 0, 2048, 4096, 5120, 6144, 7168
COL_Q, COL_K, COL_V, COL_BC = 10240, 11264, 12288, 13312
P_COLS = 13824

VMEM_LIMIT = 56 * 1024 * 1024


def _cparams(sem):
    return pltpu.CompilerParams(dimension_semantics=sem, vmem_limit_bytes=VMEM_LIMIT)


def _sigmoid(x):
    return 0.5 + 0.5 * jnp.tanh(0.5 * x)


def _silu(x):
    hx = 0.5 * x
    return hx + hx * jnp.tanh(hx)


def _softplus(x):
    return jnp.maximum(x, 0.0) + jnp.log1p(jnp.exp(-jnp.abs(x)))


def _split_bf16(x, n):
    parts, r = [], x
    for _ in range(n):
        p = r.astype(BF16)
        parts.append(p)
        r = r - p.astype(F32)
    return parts


def _rms_scale(x, w):
    var = jnp.mean(x * x, axis=-1, keepdims=True)
    return (x * lax.rsqrt(var + EPS)) * w


def _inproj_kernel(x_ref, nw_ref, w_ref, wdt_ref, o_ref, dt_ref, h_ref):
    @pl.when(pl.program_id(1) == 0)
    def _():
        hb = _rms_scale(x_ref[...], nw_ref[...]).astype(BF16)
        h_ref[...] = hb
        dt_ref[...] = jnp.dot(hb, wdt_ref[...], preferred_element_type=F32)

    o_ref[...] = jnp.dot(h_ref[...], w_ref[...], preferred_element_type=F32).astype(o_ref.dtype)


def _inproj(x2d, nw, w, wdt, tm, tn):
    M, D = x2d.shape
    N = w.shape[1]
    return pl.pallas_call(
        _inproj_kernel,
        grid=(M // tm, N // tn),
        in_specs=[
            pl.BlockSpec((tm, D), lambda i, j: (i, 0)),
            pl.BlockSpec((1, D), lambda i, j: (0, 0)),
            pl.BlockSpec((D, tn), lambda i, j: (0, j)),
            pl.BlockSpec((D, LANES), lambda i, j: (0, 0)),
        ],
        out_specs=[pl.BlockSpec((tm, tn), lambda i, j: (i, j)),
                   pl.BlockSpec((tm, LANES), lambda i, j: (i, 0))],
        out_shape=[jax.ShapeDtypeStruct((M, N), BF16), jax.ShapeDtypeStruct((M, LANES), F32)],
        scratch_shapes=[pltpu.VMEM((tm, D), BF16)],
        compiler_params=_cparams(("parallel", "arbitrary")),
        name="inproj",
    )(x2d, nw, w, wdt)


def _ssm_kernel(z_ref, xs_ref, bc_ref, dt_ref, cwx_ref, cbx_ref, cwbc_ref, cbbc_ref, dtb_ref,
                alog_ref, dskip_ref, nw_ref, tri_ref, exp_ref, o_ref,
                extx_ref, extbc_ref, state_ref):
    c = pl.program_id(1)
    L = CHUNK
    H0 = CONV_HALO

    @pl.when(c == 0)
    def _():
        extx_ref[0:H0, :] = jnp.zeros((H0, SSM_WIDTH), F32)
        extbc_ref[0:H0, :] = jnp.zeros((H0, BC_WIDTH), F32)
        state_ref[...] = jnp.zeros_like(state_ref)

    def conv(u_ref, ext_ref, w_ref, b_ref):
        ext_ref[H0:H0 + L, :] = u_ref[0].astype(F32)
        acc = b_ref[...] + w_ref[CONV_WIDTH - 1:CONV_WIDTH, :] * ext_ref[H0:H0 + L, :]
        for k in range(CONV_WIDTH - 1):
            s = CONV_WIDTH - 1 - k
            acc = acc + w_ref[k:k + 1, :] * ext_ref[H0 - s:H0 - s + L, :]
        ext_ref[0:H0, :] = ext_ref[L:L + H0, :]
        return _silu(acc)

    xs = conv(xs_ref, extx_ref, cwx_ref, cbx_ref)
    bc = conv(bc_ref, extbc_ref, cwbc_ref, cbbc_ref)

    dt = _softplus(dt_ref[0] + dtb_ref[...])
    a = -jnp.exp(alog_ref[...])
    adt = dt * a
    tri = tri_ref[...]
    a_cum = sum(jnp.dot(tri, p, preferred_element_type=F32) for p in _split_bf16(adt, 3))
    a_cum_t = a_cum.T
    a_last = a_cum[L - 1:L, :]

    stack = jnp.concatenate([dt, jnp.exp(a_last - a_cum), jnp.exp(a_cum)], axis=0)
    expanded = jnp.dot(jnp.concatenate(_split_bf16(stack, 2), axis=1), exp_ref[...],
                       preferred_element_type=F32)
    dt_x = expanded[0:L]
    dec_x = expanded[L:2 * L]
    expa_x = expanded[2 * L:3 * L]

    xdt = xs * dt_x
    dxs = (xdt * dec_x).astype(BF16)

    row = lax.broadcasted_iota(jnp.int32, (L, L), 0)
    col = lax.broadcasted_iota(jnp.int32, (L, L), 1)
    causal = row >= col
    lane = lax.broadcasted_iota(jnp.int32, (L, LANES), 1)
    first_half = lane < SSM_HEAD_DIM

    hpg = SSM_HEADS // SSM_GROUPS
    gw = hpg * SSM_HEAD_DIM
    y_parts = []
    for g in range(SSM_GROUPS):
        b_g = bc[:, g * SSM_STATE:(g + 1) * SSM_STATE]
        c_g = bc[:, (SSM_GROUPS + g) * SSM_STATE:(SSM_GROUPS + g + 1) * SSM_STATE].astype(BF16)
        cb = lax.dot_general(c_g, b_g.astype(BF16), (((1,), (1,)), ((), ())),
                             preferred_element_type=F32)
        cb = jnp.where(causal, cb, 0.0)
        for jj in range(hpg // 2):
            pj = g * (hpg // 2) + jj
            ms = []
            for h in (2 * pj, 2 * pj + 1):
                seg = jnp.minimum(a_cum[:, h:h + 1] - a_cum_t[h:h + 1, :], 0.0)
                ms.append((cb * jnp.exp(seg)).astype(BF16))
            lhs = jnp.concatenate(ms, axis=1)
            xp = xdt[:, pj * LANES:(pj + 1) * LANES]
            rhs = jnp.concatenate([jnp.where(first_half, xp, 0.0).astype(BF16),
                                   jnp.where(first_half, 0.0, xp).astype(BF16)], axis=0)
            y_parts.append(jnp.dot(lhs, rhs, preferred_element_type=F32))
    y = jnp.concatenate(y_parts, axis=1)

    offs = []
    for g in range(SSM_GROUPS):
        b_g_t = bc[:, g * SSM_STATE:(g + 1) * SSM_STATE].T.astype(BF16)
        c_g = bc[:, (SSM_GROUPS + g) * SSM_STATE:(SSM_GROUPS + g + 1) * SSM_STATE].astype(BF16)
        prev = state_ref[g]
        offs.append(jnp.dot(c_g, prev.astype(BF16), preferred_element_type=F32))
        new = jnp.dot(b_g_t, dxs[:, g * gw:(g + 1) * gw], preferred_element_type=F32)
        state_ref[g] = prev * expa_x[L - 1:L, g * gw:(g + 1) * gw] + new
    y = y + jnp.concatenate(offs, axis=1) * expa_x + xs * dskip_ref[...]

    v = y * _silu(z_ref[0].astype(F32))
    o_ref[0] = _rms_scale(v, nw_ref[...]).astype(o_ref.dtype)


def _ssm_branch(P, dtraw, cw, cb, dtb, alog, dskip_x, nw, tri, expand2):
    B, S, _ = P.shape
    L = CHUNK
    cwx, cwbc = cw[:, :SSM_WIDTH], cw[:, SSM_WIDTH:]
    cbx, cbbc = cb[:, :SSM_WIDTH], cb[:, SSM_WIDTH:]
    const = lambda shape: pl.BlockSpec(shape, lambda b, c: (0,) * len(shape))
    return pl.pallas_call(
        _ssm_kernel,
        grid=(B, S // L),
        in_specs=[
            pl.BlockSpec((1, L, SSM_WIDTH), lambda b, c: (b, c, COL_Z // SSM_WIDTH)),
            pl.BlockSpec((1, L, SSM_WIDTH), lambda b, c: (b, c, COL_XS // SSM_WIDTH)),
            pl.BlockSpec((1, L, BC_WIDTH), lambda b, c: (b, c, COL_BC // BC_WIDTH)),
            pl.BlockSpec((1, L, LANES), lambda b, c: (b, c, 0)),
            const((CONV_WIDTH, SSM_WIDTH)), const((1, SSM_WIDTH)),
            const((CONV_WIDTH, BC_WIDTH)), const((1, BC_WIDTH)),
            const((1, LANES)), const((1, LANES)), const((1, SSM_WIDTH)), const((1, SSM_WIDTH)),
            const((L, L)), const((2 * LANES, SSM_WIDTH)),
        ],
        out_specs=pl.BlockSpec((1, L, SSM_WIDTH), lambda b, c: (b, c, 0)),
        out_shape=jax.ShapeDtypeStruct((B, S, SSM_WIDTH), BF16),
        scratch_shapes=[
            pltpu.VMEM((CONV_HALO + L, SSM_WIDTH), F32),
            pltpu.VMEM((CONV_HALO + L, BC_WIDTH), F32),
            pltpu.VMEM((SSM_GROUPS, SSM_STATE, SSM_WIDTH // SSM_GROUPS), F32),
        ],
        compiler_params=_cparams(("parallel", "arbitrary")),
        name="ssm_branch",
    )(P, P, P, dtraw, cwx, cbx, cwbc, cbbc, dtb, alog, dskip_x, nw, tri, expand2)


def _pool_kernel(u_ref, g_ref, w_ref, sc_ref, o_ref, ext_ref, *, T):
    t = pl.program_id(1)
    H0 = POOL_HALO

    @pl.when(t == 0)
    def _():
        ext_ref[0:H0, :] = jnp.zeros((H0, POOL_WIDTH), F32)

    ext_ref[H0:H0 + T, :] = u_ref[0].astype(F32)
    pos = t * T + lax.broadcasted_iota(jnp.int32, (T, 1), 0)
    outs = []
    for g, w in enumerate(POOL_WINDOWS):
        cs = slice(g * POOL_GROUP_DIM, (g + 1) * POOL_GROUP_DIM)
        u = ext_ref[H0:H0 + T, cs]
        win = u
        for k in range(1, w):
            win = win + ext_ref[H0 - k:H0 - k + T, cs]
        cnt = jnp.minimum(pos + 1, w).astype(F32)
        mixed = (win / cnt - u).astype(BF16)
        outs.append(jnp.dot(mixed, w_ref[g], preferred_element_type=F32))
    ext_ref[0:H0, :] = ext_ref[T:T + H0, :]
    mixed = jnp.concatenate(outs, axis=1)
    o_ref[0] = (mixed * sc_ref[...] * _silu(g_ref[0].astype(F32))).astype(o_ref.dtype)


def _pool_branch(P, pool_w, pool_scale, T):
    B, S, _ = P.shape
    return pl.pallas_call(
        functools.partial(_pool_kernel, T=T),
        grid=(B, S // T),
        in_specs=[
            pl.BlockSpec((1, T, POOL_WIDTH), lambda b, t: (b, t, COL_PU // POOL_WIDTH)),
            pl.BlockSpec((1, T, POOL_WIDTH), lambda b, t: (b, t, COL_PG // POOL_WIDTH)),
            pl.BlockSpec((len(POOL_WINDOWS), POOL_GROUP_DIM, POOL_GROUP_DIM), lambda b, t: (0, 0, 0)),
            pl.BlockSpec((1, POOL_WIDTH), lambda b, t: (0, 0)),
        ],
        out_specs=pl.BlockSpec((1, T, POOL_WIDTH), lambda b, t: (b, t, 0)),
        out_shape=jax.ShapeDtypeStruct((B, S, POOL_WIDTH), BF16),
        scratch_shapes=[pltpu.VMEM((POOL_HALO + T, POOL_WIDTH), F32)],
        compiler_params=_cparams(("parallel", "arbitrary")),
        name="pool_branch",
    )(P, P, pool_w, pool_scale)


def _attn_kernel(q_ref, k_ref, v_ref, g_ref, perm_ref, o_ref, kp_ref, vt_ref, acc_ref, *, T, S):
    i = pl.program_id(2)
    nv = T // SUBLANES
    nh = SB_STEP_HEADS
    hl = nh * SB_HEAD_DIM

    @pl.when(i == 0)
    def _():
        pm = perm_ref[...]
        for blk in range(S // T):
            rows = slice(blk * T, (blk + 1) * T)
            kp_ref[blk] = jnp.dot(pm, k_ref[0, rows, :], preferred_element_type=F32).astype(BF16)
            vp = jnp.dot(pm, v_ref[0, rows, :], preferred_element_type=F32).astype(BF16)
            vt_ref[blk] = vp.astype(F32).T.astype(BF16)

    first = lax.broadcasted_iota(jnp.int32, (T, LANES), 1) < SB_HEAD_DIM
    q = q_ref[0].astype(F32) * (0.5 * SB_HEAD_DIM ** -0.5)
    q_heads = []
    for h in range(nh):
        qp = q[:, (h // 2) * LANES:(h // 2 + 1) * LANES]
        q_heads.append((jnp.where(first, qp, 0.0) if h % 2 == 0 else jnp.where(first, 0.0, qp)).astype(BF16))
    sub = lax.broadcasted_iota(jnp.int32, (SUBLANES, T), 0)
    q_idx = lax.broadcasted_iota(jnp.int32, (SUBLANES, T), 1)
    key_base = sub * nv

    def head_tile(zt, carry, diag):
        acc = jnp.ones((SUBLANES, T), F32)
        ws = [None] * nv
        for v in reversed(range(nv)):
            omb = 0.5 - 0.5 * jnp.tanh(zt[v * SUBLANES:(v + 1) * SUBLANES, :])
            if diag:
                omb = jnp.where(key_base + v < q_idx, omb, 1.0)
            nxt = acc * omb
            ws[v] = acc - nxt
            acc = nxt
        exc = jnp.where(sub + 1 < SUBLANES, pltpu.roll(acc, SUBLANES - 1, axis=0), 1.0)
        for sh in (1, 2, 4):
            exc = exc * jnp.where(sub + sh < SUBLANES, pltpu.roll(exc, SUBLANES - sh, axis=0), 1.0)
        offb = exc * carry
        att_t = jnp.concatenate([ws[v] * offb for v in range(nv)], axis=0).astype(BF16)
        return att_t, carry * (acc[0:1, :] * exc[0:1, :])

    def tile(j, carries, diag):
        kp = kp_ref[j]
        zts = [lax.dot_general(kp[:, (h // 2) * LANES:(h // 2 + 1) * LANES], q_heads[h],
                               (((1,), (1,)), ((), ())), preferred_element_type=F32) for h in range(nh)]
        new_c = []
        for h in range(nh):
            att_t, c = head_tile(zts[h], carries[h], diag)
            rows = slice(h * SB_HEAD_DIM, (h + 1) * SB_HEAD_DIM)
            upd = jnp.dot(vt_ref[j, rows, :], att_t, preferred_element_type=F32)
            acc_ref[rows, :] = upd if diag else acc_ref[rows, :] + upd
            new_c.append(c)
        return tuple(new_c)

    def live(carries):
        m = carries[0]
        for c in carries[1:]:
            m = jnp.maximum(m, c)
        return (jnp.max(m) > 0.0).astype(jnp.int32)

    carries = tile(i, tuple(jnp.ones((1, T), F32) for _ in range(nh)), True)

    def cond(st):
        return jnp.logical_and(st[0] >= 0, st[2] > 0)

    def body(st):
        c = tile(st[0], st[1], False)
        return st[0] - 1, c, live(c)

    lax.while_loop(cond, body, (i - 1, carries, live(carries)))
    o = acc_ref[...].T
    o_ref[0] = (o * _silu(g_ref[0].astype(F32))).astype(o_ref.dtype)


def _attn_branch(P, perm):
    B, S, _ = P.shape
    T = SB_TILE
    hl = SB_STEP_HEADS * SB_HEAD_DIM
    nblk = SB_WIDTH // hl
    return pl.pallas_call(
        functools.partial(_attn_kernel, T=T, S=S),
        grid=(B, nblk, S // T),
        in_specs=[
            pl.BlockSpec((1, T, hl), lambda b, p, i: (b, i, COL_Q // hl + p)),
            pl.BlockSpec((1, S, hl), lambda b, p, i: (b, 0, COL_K // hl + p)),
            pl.BlockSpec((1, S, hl), lambda b, p, i: (b, 0, COL_V // hl + p)),
            pl.BlockSpec((1, T, hl), lambda b, p, i: (b, i, COL_SG // hl + p)),
            pl.BlockSpec((T, T), lambda b, p, i: (0, 0)),
        ],
        out_specs=pl.BlockSpec((1, T, hl), lambda b, p, i: (b, i, p)),
        out_shape=jax.ShapeDtypeStruct((B, S, SB_WIDTH), BF16),
        scratch_shapes=[pltpu.VMEM((S // T, T, hl), BF16), pltpu.VMEM((S // T, hl, T), BF16),
                        pltpu.VMEM((hl, T), F32)],
        compiler_params=_cparams(("parallel", "parallel", "arbitrary")),
        name="sb_attention",
    )(P, P, P, P, perm)


def _merge_kernel(x_ref, ys_ref, yp_ref, yb_ref, m0_ref, m1_ref, m2_ref, ws_ref, wp_ref, wb_ref,
                  wo_ref, fw_ref, o_ref, *, final_norm):
    gate = lambda m_ref: _sigmoid(m_ref[...].astype(F32))
    merged = gate(m0_ref) * jnp.dot(ys_ref[...], ws_ref[...], preferred_element_type=F32)
    merged = merged + gate(m1_ref) * jnp.dot(yp_ref[...], wp_ref[...], preferred_element_type=F32)
    merged = merged + gate(m2_ref) * jnp.dot(yb_ref[...], wb_ref[...], preferred_element_type=F32)
    out = x_ref[...] + jnp.dot(merged.astype(BF16), wo_ref[...], preferred_element_type=F32)
    if final_norm:
        out = _rms_scale(out, fw_ref[...])
    o_ref[...] = out


def _merge(x2d, ys, yp, yb, P2d, ws, wp, wb, wo, fw, final_norm, tm):
    M, D = x2d.shape
    rows = lambda w, cb: pl.BlockSpec((tm, w), lambda i: (i, cb))
    const = lambda shape: pl.BlockSpec(shape, lambda i: (0, 0))
    m_blk = COL_M0 // D
    return pl.pallas_call(
        functools.partial(_merge_kernel, final_norm=final_norm),
        grid=(M // tm,),
        in_specs=[
            rows(D, 0), rows(SSM_WIDTH, 0), rows(POOL_WIDTH, 0), rows(SB_WIDTH, 0),
            rows(D, m_blk), rows(D, m_blk + 1), rows(D, m_blk + 2),
            const((SSM_WIDTH, D)), const((POOL_WIDTH, D)), const((SB_WIDTH, D)), const((D, D)),
            const((1, D)),
        ],
        out_specs=rows(D, 0),
        out_shape=jax.ShapeDtypeStruct((M, D), F32),
        compiler_params=_cparams(("parallel",)),
        name="merge_outproj",
    )(x2d, ys, yp, yb, P2d, P2d, P2d, ws, wp, wb, wo, fw)


def kernel(x, norm_w, w_in, conv_w, conv_b, dt_bias, a_log, d_skip, ssm_norm_w, pool_w, pool_scale,
           w_proj_ssm, w_proj_pool, w_proj_sb, w_out, final_norm_w):
    B, S, D = x.shape
    depth = w_in.shape[0]
    M = B * S
    pool_t = 512

    c_xbc = SSM_WIDTH
    c_dt = c_xbc + SSM_WIDTH + BC_WIDTH
    c_pu = c_dt + SSM_HEADS
    c_qkv = c_pu + 2 * POOL_WIDTH
    c_sg = c_qkv + 3 * SB_WIDTH
    w_p = jnp.concatenate([w_in[:, :, 0:c_xbc + SSM_WIDTH], w_in[:, :, c_pu:c_qkv], w_in[:, :, c_sg:],
                           w_in[:, :, c_qkv:c_sg], w_in[:, :, c_xbc + SSM_WIDTH:c_dt]], axis=2).astype(BF16)
    w_dt =jnp.pad(w_in[:, :, c_dt:c_pu], ((0, 0), (0, 0), (0, LANES - SSM_HEADS))).astype(BF16)
    pad_h = lambda a: jnp.pad(a, ((0, 0), (0, LANES - SSM_HEADS)))[:, None, :]
    dtb, alog = pad_h(dt_bias), pad_h(a_log)
    dskip_x = jnp.repeat(d_skip, SSM_HEAD_DIM, axis=1)[:, None, :]
    ws, wp, wb, wo = (w.astype(BF16) for w in (w_proj_ssm, w_proj_pool, w_proj_sb, w_out))
    pw = pool_w.astype(BF16)

    r = jnp.arange(CHUNK)
    tri = (r[None, :] <= r[:, None]).astype(BF16)
    hh = jnp.arange(LANES)[:, None]
    expand = ((jnp.arange(SSM_WIDTH)[None, :] // SSM_HEAD_DIM) == hh).astype(BF16)
    expand2 = jnp.concatenate([expand, expand], axis=0)
    rr = jnp.arange(SB_TILE)
    strip_key = (rr % SUBLANES) * (SB_TILE // SUBLANES) + rr // SUBLANES
    perm = (strip_key[:, None] == rr[None, :]).astype(BF16)

    x2d = x.reshape(M, D)
    for l in range(depth):
        P2d, dtraw = _inproj(x2d, norm_w[l][None, :], w_p[l], w_dt[l], min(2048, M), 1536)
        P = P2d.reshape(B, S, P_COLS)
        ys =_ssm_branch(P, dtraw.reshape(B, S, LANES), conv_w[l], conv_b[l][None, :], dtb[l], alog[l],
                         dskip_x[l], ssm_norm_w[l][None, :], tri, expand2)
        yp = _pool_branch(P, pw[l], pool_scale[l][None, :], pool_t)
        yb = _attn_branch(P, perm)
        x2d = _merge(x2d, ys.reshape(M, SSM_WIDTH), yp.reshape(M, POOL_WIDTH), yb.reshape(M, SB_WIDTH),
                     P2d, ws[l], wp[l], wb[l], wo[l], final_norm_w[None, :], l == depth - 1, 512)
    return x2d.reshape(B, S, D)
```

```python
import functools

import jax
import jax.numpy as jnp
from jax import lax
from jax.experimental import pallas as pl
from jax.experimental.pallas import tpu as pltpu

F32 = jnp.float32
BF16 = jnp.bfloat16

D_MODEL = 1024
SSM_WIDTH = 2048
SSM_HEAD_DIM = 64
SSM_HEADS = 32
SSM_GROUPS = 2
SSM_STATE = 128
CONV_WIDTH = 4
CHUNK = 128
BC_WIDTH = 2 * SSM_GROUPS * SSM_STATE
POOL_WIDTH = 1024
POOL_WINDOWS = (2, 4, 8, 16)
POOL_GROUP_DIM = 256
POOL_HALO = 16
SB_WIDTH = 1024
SB_HEAD_DIM = 64
SB_TILE = 256
SB_STEP_HEADS = 16
EPS = 1e-6
LANES = 128
SUBLANES = 8
CONV_HALO = 8

COL_Z, COL_XS, COL_PU, COL_PG, COL_SG, COL_M0 = 0, 2048, 4096, 5120, 6144, 7168
COL_Q, COL_K, COL_V, COL_BC = 10240, 11264, 12288, 13312
P_COLS = 13824

VMEM_LIMIT = 56 * 1024 * 1024


def _cparams(sem):
    return pltpu.CompilerParams(dimension_semantics=sem, vmem_limit_bytes=VMEM_LIMIT)


def _sigmoid(x):
    return 0.5 + 0.5 * jnp.tanh(0.5 * x)


def _silu(x):
    hx = 0.5 * x
    return hx + hx * jnp.tanh(hx)


def _softplus(x):
    return jnp.maximum(x, 0.0) + jnp.log1p(jnp.exp(-jnp.abs(x)))


def _split_bf16(x, n):
    parts, r = [], x
    for _ in range(n):
        p = r.astype(BF16)
        parts.append(p)
        r = r - p.astype(F32)
    return parts


def _rms_scale(x, w):
    var = jnp.mean(x * x, axis=-1, keepdims=True)
    return (x * lax.rsqrt(var + EPS)) * w


def _inproj_kernel(x_ref, nw_ref, w_ref, wdt_ref, o_ref, dt_ref, h_ref):
    @pl.when(pl.program_id(1) == 0)
    def _():
        hb = _rms_scale(x_ref[...], nw_ref[...]).astype(BF16)
        h_ref[...] = hb
        dt_ref[...] = jnp.dot(hb, wdt_ref[...], preferred_element_type=F32)

    o_ref[...] = jnp.dot(h_ref[...], w_ref[...], preferred_element_type=F32).astype(o_ref.dtype)


def _inproj(x2d, nw, w, wdt, tm, tn):
    M, D = x2d.shape
    N = w.shape[1]
    return pl.pallas_call(
        _inproj_kernel,
        grid=(M // tm, N // tn),
        in_specs=[
            pl.BlockSpec((tm, D), lambda i, j: (i, 0)),
            pl.BlockSpec((1, D), lambda i, j: (0, 0)),
            pl.BlockSpec((D, tn), lambda i, j: (0, j)),
            pl.BlockSpec((D, LANES), lambda i, j: (0, 0)),
        ],
        out_specs=[pl.BlockSpec((tm, tn), lambda i, j: (i, j)),
                   pl.BlockSpec((tm, LANES), lambda i, j: (i, 0))],
        out_shape=[jax.ShapeDtypeStruct((M, N), BF16), jax.ShapeDtypeStruct((M, LANES), F32)],
        scratch_shapes=[pltpu.VMEM((tm, D), BF16)],
        compiler_params=_cparams(("parallel", "arbitrary")),
        name="inproj",
    )(x2d, nw, w, wdt)


def _ssm_kernel(z_ref, xs_ref, bc_ref, dt_ref, cwx_ref, cbx_ref, cwbc_ref, cbbc_ref, dtb_ref,
                alog_ref, dskip_ref, nw_ref, tri_ref, exp_ref, o_ref,
                extx_ref, extbc_ref, state_ref):
    c = pl.program_id(1)
    L = CHUNK
    H0 = CONV_HALO

    @pl.when(c == 0)
    def _():
        extx_ref[0:H0, :] = jnp.zeros((H0, SSM_WIDTH), F32)
        extbc_ref[0:H0, :] = jnp.zeros((H0, BC_WIDTH), F32)
        state_ref[...] = jnp.zeros_like(state_ref)

    def conv(u_ref, ext_ref, w_ref, b_ref):
        ext_ref[H0:H0 + L, :] = u_ref[0].astype(F32)
        acc = b_ref[...] + w_ref[CONV_WIDTH - 1:CONV_WIDTH, :] * ext_ref[H0:H0 + L, :]
        for k in range(CONV_WIDTH - 1):
            s = CONV_WIDTH - 1 - k
            acc = acc + w_ref[k:k + 1, :] * ext_ref[H0 - s:H0 - s + L, :]
        ext_ref[0:H0, :] = ext_ref[L:L + H0, :]
        return _silu(acc)

    xs = conv(xs_ref, extx_ref, cwx_ref, cbx_ref)
    bc = conv(bc_ref, extbc_ref, cwbc_ref, cbbc_ref)

    dt = _softplus(dt_ref[0] + dtb_ref[...])
    a = -jnp.exp(alog_ref[...])
    adt = dt * a
    tri = tri_ref[...]
    a_cum = sum(jnp.dot(tri, p, preferred_element_type=F32) for p in _split_bf16(adt, 3))
    a_cum_t = a_cum.T
    a_last = a_cum[L - 1:L, :]

    stack = jnp.concatenate([dt, jnp.exp(a_last - a_cum), jnp.exp(a_cum)], axis=0)
    expanded = jnp.dot(jnp.concatenate(_split_bf16(stack, 2), axis=1), exp_ref[...],
                       preferred_element_type=F32)
    dt_x = expanded[0:L]
    dec_x = expanded[L:2 * L]
    expa_x = expanded[2 * L:3 * L]

    xdt = xs * dt_x
    dxs = (xdt * dec_x).astype(BF16)

    row = lax.broadcasted_iota(jnp.int32, (L, L), 0)
    col = lax.broadcasted_iota(jnp.int32, (L, L), 1)
    causal = row >= col
    lane = lax.broadcasted_iota(jnp.int32, (L, LANES), 1)
    first_half = lane < SSM_HEAD_DIM

    hpg = SSM_HEADS // SSM_GROUPS
    gw = hpg * SSM_HEAD_DIM
    y_parts = []
    for g in range(SSM_GROUPS):
        b_g = bc[:, g * SSM_STATE:(g + 1) * SSM_STATE]
        c_g = bc[:, (SSM_GROUPS + g) * SSM_STATE:(SSM_GROUPS + g + 1) * SSM_STATE].astype(BF16)
        cb = lax.dot_general(c_g, b_g.astype(BF16), (((1,), (1,)), ((), ())),
                             preferred_element_type=F32)
        cb = jnp.where(causal, cb, 0.0)
        for jj in range(hpg // 2):
            pj = g * (hpg // 2) + jj
            ms = []
            for h in (2 * pj, 2 * pj + 1):
                seg = jnp.minimum(a_cum[:, h:h + 1] - a_cum_t[h:h + 1, :], 0.0)
                ms.append((cb * jnp.exp(seg)).astype(BF16))
            lhs = jnp.concatenate(ms, axis=1)
            xp = xdt[:, pj * LANES:(pj + 1) * LANES]
            rhs = jnp.concatenate([jnp.where(first_half, xp, 0.0).astype(BF16),
                                   jnp.where(first_half, 0.0, xp).astype(BF16)], axis=0)
            y_parts.append(jnp.dot(lhs, rhs, preferred_element_type=F32))
    y = jnp.concatenate(y_parts, axis=1)

    offs = []
    for g in range(SSM_GROUPS):
        b_g_t = bc[:, g * SSM_STATE:(g + 1) * SSM_STATE].T.astype(BF16)
        c_g = bc[:, (SSM_GROUPS + g) * SSM_STATE:(SSM_GROUPS + g + 1) * SSM_STATE].astype(BF16)
        prev = state_ref[g]
        offs.append(jnp.dot(c_g, prev.astype(BF16), preferred_element_type=F32))
        new = jnp.dot(b_g_t, dxs[:, g * gw:(g + 1) * gw], preferred_element_type=F32)
        state_ref[g] = prev * expa_x[L - 1:L, g * gw:(g + 1) * gw] + new
    y = y + jnp.concatenate(offs, axis=1) * expa_x + xs * dskip_ref[...]

    v = y * _silu(z_ref[0].astype(F32))
    o_ref[0] = _rms_scale(v, nw_ref[...]).astype(o_ref.dtype)


def _ssm_branch(P, dtraw, cw, cb, dtb, alog, dskip_x, nw, tri, expand2):
    B, S, _ = P.shape
    L = CHUNK
    cwx, cwbc = cw[:, :SSM_WIDTH], cw[:, SSM_WIDTH:]
    cbx, cbbc = cb[:, :SSM_WIDTH], cb[:, SSM_WIDTH:]
    const = lambda shape: pl.BlockSpec(shape, lambda b, c: (0,) * len(shape))
    return pl.pallas_call(
        _ssm_kernel,
        grid=(B, S // L),
        in_specs=[
            pl.BlockSpec((1, L, SSM_WIDTH), lambda b, c: (b, c, COL_Z // SSM_WIDTH)),
            pl.BlockSpec((1, L, SSM_WIDTH), lambda b, c: (b, c, COL_XS // SSM_WIDTH)),
            pl.BlockSpec((1, L, BC_WIDTH), lambda b, c: (b, c, COL_BC // BC_WIDTH)),
            pl.BlockSpec((1, L, LANES), lambda b, c: (b, c, 0)),
            const((CONV_WIDTH, SSM_WIDTH)), const((1, SSM_WIDTH)),
            const((CONV_WIDTH, BC_WIDTH)), const((1, BC_WIDTH)),
            const((1, LANES)), const((1, LANES)), const((1, SSM_WIDTH)), const((1, SSM_WIDTH)),
            const((L, L)), const((2 * LANES, SSM_WIDTH)),
        ],
        out_specs=pl.BlockSpec((1, L, SSM_WIDTH), lambda b, c: (b, c, 0)),
        out_shape=jax.ShapeDtypeStruct((B, S, SSM_WIDTH), BF16),
        scratch_shapes=[
            pltpu.VMEM((CONV_HALO + L, SSM_WIDTH), F32),
            pltpu.VMEM((CONV_HALO + L, BC_WIDTH), F32),
            pltpu.VMEM((SSM_GROUPS, SSM_STATE, SSM_WIDTH // SSM_GROUPS), F32),
        ],
        compiler_params=_cparams(("parallel", "arbitrary")),
        name="ssm_branch",
    )(P, P, P, dtraw, cwx, cbx, cwbc, cbbc, dtb, alog, dskip_x, nw, tri, expand2)


def _pool_kernel(u_ref, g_ref, w_ref, sc_ref, o_ref, ext_ref, *, T):
    t = pl.program_id(1)
    H0 = POOL_HALO

    @pl.when(t == 0)
    def _():
        ext_ref[0:H0, :] = jnp.zeros((H0, POOL_WIDTH), F32)

    ext_ref[H0:H0 + T, :] = u_ref[0].astype(F32)
    pos = t * T + lax.broadcasted_iota(jnp.int32, (T, 1), 0)
    outs = []
    for g, w in enumerate(POOL_WINDOWS):
        cs = slice(g * POOL_GROUP_DIM, (g + 1) * POOL_GROUP_DIM)
        s = ext_ref[:, cs]
        u = s[H0:]
        d = 1
        while d < w:
            s = s + pltpu.roll(s, d, axis=0)
            d *= 2
        win = s[H0:]
        cnt = jnp.minimum(pos + 1, w).astype(F32)
        mixed = (win / cnt - u).astype(BF16)
        outs.append(jnp.dot(mixed, w_ref[g], preferred_element_type=F32))
    ext_ref[0:H0, :] = ext_ref[T:T + H0, :]
    mixed = jnp.concatenate(outs, axis=1)
    o_ref[0] = (mixed * sc_ref[...] * _silu(g_ref[0].astype(F32))).astype(o_ref.dtype)


def _pool_branch(P, pool_w, pool_scale, T):
    B, S, _ = P.shape
    return pl.pallas_call(
        functools.partial(_pool_kernel, T=T),
        grid=(B, S // T),
        in_specs=[
            pl.BlockSpec((1, T, POOL_WIDTH), lambda b, t: (b, t, COL_PU // POOL_WIDTH)),
            pl.BlockSpec((1, T, POOL_WIDTH), lambda b, t: (b, t, COL_PG // POOL_WIDTH)),
            pl.BlockSpec((len(POOL_WINDOWS), POOL_GROUP_DIM, POOL_GROUP_DIM), lambda b, t: (0, 0, 0)),
            pl.BlockSpec((1, POOL_WIDTH), lambda b, t: (0, 0)),
        ],
        out_specs=pl.BlockSpec((1, T, POOL_WIDTH), lambda b, t: (b, t, 0)),
        out_shape=jax.ShapeDtypeStruct((B, S, POOL_WIDTH), BF16),
        scratch_shapes=[pltpu.VMEM((POOL_HALO + T, POOL_WIDTH), F32)],
        compiler_params=_cparams(("parallel", "arbitrary")),
        name="pool_branch",
    )(P, P, pool_w, pool_scale)


def _attn_kernel(q_ref, k_ref, v_ref, g_ref, perm_ref, o_ref, kp_ref, vt_ref, acc_ref, *, T, S):
    i = pl.program_id(2)
    nv = T // SUBLANES
    nh = SB_STEP_HEADS
    hl = nh * SB_HEAD_DIM

    @pl.when(i == 0)
    def _():
        pm = perm_ref[...]
        for blk in range(S // T):
            rows = slice(blk * T, (blk + 1) * T)
            kp_ref[blk] = jnp.dot(pm, k_ref[0, rows, :], preferred_element_type=F32).astype(BF16)
            vp = jnp.dot(pm, v_ref[0, rows, :], preferred_element_type=F32).astype(BF16)
            vt_ref[blk] = vp.astype(F32).T.astype(BF16)

    first = lax.broadcasted_iota(jnp.int32, (T, LANES), 1) < SB_HEAD_DIM
    q = q_ref[0].astype(F32) * (0.5 * SB_HEAD_DIM ** -0.5)
    q_heads = []
    for h in range(nh):
        qp = q[:, (h // 2) * LANES:(h // 2 + 1) * LANES]
        q_heads.append((jnp.where(first, qp, 0.0) if h % 2 == 0 else jnp.where(first, 0.0, qp)).astype(BF16))
    sub = lax.broadcasted_iota(jnp.int32, (SUBLANES, T), 0)
    q_idx = lax.broadcasted_iota(jnp.int32, (SUBLANES, T), 1)
    key_base = sub * nv

    def head_tile(zt, carry, diag):
        acc = jnp.ones((SUBLANES, T), F32)
        ws = [None] * nv
        for v in reversed(range(nv)):
            omb = 0.5 - 0.5 * jnp.tanh(zt[v * SUBLANES:(v + 1) * SUBLANES, :])
            if diag:
                omb = jnp.where(key_base + v < q_idx, omb, 1.0)
            nxt = acc * omb
            ws[v] = acc - nxt
            acc = nxt
        exc = jnp.where(sub + 1 < SUBLANES, pltpu.roll(acc, SUBLANES - 1, axis=0), 1.0)
        for sh in (1, 2, 4):
            exc = exc * jnp.where(sub + sh < SUBLANES, pltpu.roll(exc, SUBLANES - sh, axis=0), 1.0)
        offb = exc * carry
        att_t = jnp.concatenate([ws[v] * offb for v in range(nv)], axis=0).astype(BF16)
        return att_t, carry * (acc[0:1, :] * exc[0:1, :])

    def tile(j, carries, diag):
        kp = kp_ref[j]
        zts = [lax.dot_general(kp[:, (h // 2) * LANES:(h // 2 + 1) * LANES], q_heads[h],
                               (((1,), (1,)), ((), ())), preferred_element_type=F32) for h in range(nh)]
        new_c = []
        for h in range(nh):
            att_t, c = head_tile(zts[h], carries[h], diag)
            rows = slice(h * SB_HEAD_DIM, (h + 1) * SB_HEAD_DIM)
            upd = jnp.dot(vt_ref[j, rows, :], att_t, preferred_element_type=F32)
            acc_ref[rows, :] = upd if diag else acc_ref[rows, :] + upd
            new_c.append(c)
        return tuple(new_c)

    def live(carries):
        m = carries[0]
        for c in carries[1:]:
            m = jnp.maximum(m, c)
        return (jnp.max(m) > 0.0).astype(jnp.int32)

    carries = tile(i, tuple(jnp.ones((1, T), F32) for _ in range(nh)), True)

    def cond(st):
        return jnp.logical_and(st[0] >= 0, st[2] > 0)

    def body(st):
        c = tile(st[0], st[1], False)
        return st[0] - 1, c, live(c)

    lax.while_loop(cond, body, (i - 1, carries, live(carries)))
    o = acc_ref[...].T
    o_ref[0] = (o * _silu(g_ref[0].astype(F32))).astype(o_ref.dtype)


def _attn_branch(P, perm):
    B, S, _ = P.shape
    T = SB_TILE
    hl = SB_STEP_HEADS * SB_HEAD_DIM
    nblk = SB_WIDTH // hl
    return pl.pallas_call(
        functools.partial(_attn_kernel, T=T, S=S),
        grid=(B, nblk, S // T),
        in_specs=[
            pl.BlockSpec((1, T, hl), lambda b, p, i: (b, i, COL_Q // hl + p)),
            pl.BlockSpec((1, S, hl), lambda b, p, i: (b, 0, COL_K // hl + p)),
            pl.BlockSpec((1, S, hl), lambda b, p, i: (b, 0, COL_V // hl + p)),
            pl.BlockSpec((1, T, hl), lambda b, p, i: (b, i, COL_SG // hl + p)),
            pl.BlockSpec((T, T), lambda b, p, i: (0, 0)),
        ],
        out_specs=pl.BlockSpec((1, T, hl), lambda b, p, i: (b, i, p)),
        out_shape=jax.ShapeDtypeStruct((B, S, SB_WIDTH), BF16),
        scratch_shapes=[pltpu.VMEM((S // T, T, hl), BF16), pltpu.VMEM((S // T, hl, T), BF16),
                        pltpu.VMEM((hl, T), F32)],
        compiler_params=_cparams(("parallel", "parallel", "arbitrary")),
        name="sb_attention",
    )(P, P, P, P, perm)


def _merge_kernel(x_ref, ys_ref, yp_ref, yb_ref, m0_ref, m1_ref, m2_ref, ws_ref, wp_ref, wb_ref,
                  wo_ref, fw_ref, o_ref, *, final_norm):
    gate = lambda m_ref: _sigmoid(m_ref[...].astype(F32))
    merged = gate(m0_ref) * jnp.dot(ys_ref[...], ws_ref[...], preferred_element_type=F32)
    merged = merged + gate(m1_ref) * jnp.dot(yp_ref[...], wp_ref[...], preferred_element_type=F32)
    merged = merged + gate(m2_ref) * jnp.dot(yb_ref[...], wb_ref[...], preferred_element_type=F32)
    out = x_ref[...] + jnp.dot(merged.astype(BF16), wo_ref[...], preferred_element_type=F32)
    if final_norm:
        out = _rms_scale(out, fw_ref[...])
    o_ref[...] = out


def _merge(x2d, ys, yp, yb, P2d, ws, wp, wb, wo, fw, final_norm, tm):
    M, D = x2d.shape
    rows = lambda w, cb: pl.BlockSpec((tm, w), lambda i: (i, cb))
    const = lambda shape: pl.BlockSpec(shape, lambda i: (0, 0))
    m_blk = COL_M0 // D
    return pl.pallas_call(
        functools.partial(_merge_kernel, final_norm=final_norm),
        grid=(M // tm,),
        in_specs=[
            rows(D, 0), rows(SSM_WIDTH, 0), rows(POOL_WIDTH, 0), rows(SB_WIDTH, 0),
            rows(D, m_blk), rows(D, m_blk + 1), rows(D, m_blk + 2),
            const((SSM_WIDTH, D)), const((POOL_WIDTH, D)), const((SB_WIDTH, D)), const((D, D)),
            const((1, D)),
        ],
        out_specs=rows(D, 0),
        out_shape=jax.ShapeDtypeStruct((M, D), F32),
        compiler_params=_cparams(("parallel",)),
        name="merge_outproj",
    )(x2d, ys, yp, yb, P2d, P2d, P2d, ws, wp, wb, wo, fw)


def kernel(x, norm_w, w_in, conv_w, conv_b, dt_bias, a_log, d_skip, ssm_norm_w, pool_w, pool_scale,
           w_proj_ssm, w_proj_pool, w_proj_sb, w_out, final_norm_w):
    B, S, D = x.shape
    depth = w_in.shape[0]
    M = B * S
    pool_t = 512

    c_xbc = SSM_WIDTH
    c_dt = c_xbc + SSM_WIDTH + BC_WIDTH
    c_pu = c_dt + SSM_HEADS
    c_qkv = c_pu + 2 * POOL_WIDTH
    c_sg = c_qkv + 3 * SB_WIDTH
    w_p = jnp.concatenate([w_in[:, :, 0:c_xbc + SSM_WIDTH], w_in[:, :, c_pu:c_qkv], w_in[:, :, c_sg:],
                           w_in[:, :, c_qkv:c_sg], w_in[:, :, c_xbc + SSM_WIDTH:c_dt]], axis=2).astype(BF16)
    w_dt =jnp.pad(w_in[:, :, c_dt:c_pu], ((0, 0), (0, 0), (0, LANES - SSM_HEADS))).astype(BF16)
    pad_h = lambda a: jnp.pad(a, ((0, 0), (0, LANES - SSM_HEADS)))[:, None, :]
    dtb, alog = pad_h(dt_bias), pad_h(a_log)
    dskip_x = jnp.repeat(d_skip, SSM_HEAD_DIM, axis=1)[:, None, :]
    ws, wp, wb, wo = (w.astype(BF16) for w in (w_proj_ssm, w_proj_pool, w_proj_sb, w_out))
    pw = pool_w.astype(BF16)

    r = jnp.arange(CHUNK)
    tri = (r[None, :] <= r[:, None]).astype(BF16)
    hh = jnp.arange(LANES)[:, None]
    expand = ((jnp.arange(SSM_WIDTH)[None, :] // SSM_HEAD_DIM) == hh).astype(BF16)
    expand2 = jnp.concatenate([expand, expand], axis=0)
    rr = jnp.arange(SB_TILE)
    strip_key = (rr % SUBLANES) * (SB_TILE // SUBLANES) + rr // SUBLANES
    perm = (strip_key[:, None] == rr[None, :]).astype(BF16)

    x2d = x.reshape(M, D)
    for l in range(depth):
        P2d, dtraw = _inproj(x2d, norm_w[l][None, :], w_p[l], w_dt[l], min(2048, M), 1536)
        P = P2d.reshape(B, S, P_COLS)
        ys =_ssm_branch(P, dtraw.reshape(B, S, LANES), conv_w[l], conv_b[l][None, :], dtb[l], alog[l],
                         dskip_x[l], ssm_norm_w[l][None, :], tri, expand2)
        yp = _pool_branch(P, pw[l], pool_scale[l][None, :], pool_t)
        yb = _attn_branch(P, perm)
        x2d = _merge(x2d, ys.reshape(M, SSM_WIDTH), yp.reshape(M, POOL_WIDTH), yb.reshape(M, SB_WIDTH),
                     P2d, ws[l], wp[l], wb[l], wo[l], final_norm_w[None, :], l == depth - 1, 512)
    return x2d.reshape(B, S, D)
```
